```python
import jax, jax.numpy as jnp
from jax import lax
import numpy as np

D_MODEL = 1024
BATCH = 2
SEQ = 8192
DEPTH = 2

HEAD_DIM = 64
ROT_DIM = HEAD_DIM // 4
ROPE_THETA = 500000.0
RMS_EPS = 1e-6
Q_BLOCK = 128
MACARON_WEIGHT = 0.5
D_FF = 2816

A_HEADS = 8
A_KV_RANK = 128
IDX_HEADS = 8
IDX_DIM = 64
TOPK_MAX = 256

B_GROUPS = ((128, 1), (512, 4), (2048, 16))
B_HEADS_PER_GROUP = 4
B_HEADS = B_HEADS_PER_GROUP * len(B_GROUPS)

C_HEADS = 4
C_HEAD_K = 64
C_HEAD_V = 128
C_ALPHA_RANK = 16
C_GATE_TAU = 16.0
C_CHUNK = 64

N_BRANCH = 3
IN_SIZES = (A_HEADS * HEAD_DIM, A_KV_RANK, IDX_HEADS * IDX_DIM, IDX_DIM, IDX_HEADS,
            3 * B_HEADS * HEAD_DIM,
            C_HEADS * C_HEAD_K, C_HEADS * C_HEAD_K, C_HEADS * C_HEAD_V, C_HEADS * C_HEAD_V,
            C_ALPHA_RANK)
D_IN = sum(IN_SIZES)

kernel_name = 'hybrid_dsa_dilated_gla_macaron'


def rms_norm(x, g):
    xf = x.astype(jnp.float32)
    y = xf * lax.rsqrt(jnp.mean(xf * xf, axis=-1, keepdims=True) + RMS_EPS)
    return (y * g.astype(jnp.float32)).astype(x.dtype)


def swiglu(x, w_in, w_out):
    gate, up = jnp.split(x @ w_in, 2, axis=-1)
    return (jax.nn.silu(gate) * up) @ w_out


def rope_tables(positions, rot_dim):
    inv_freq = ROPE_THETA ** (-jnp.arange(0, rot_dim, 2, dtype=jnp.float32) / rot_dim)
    ang = positions.astype(jnp.float32)[..., None] * inv_freq
    return jnp.cos(ang), jnp.sin(ang)


def partial_rope(x, cos, sin):
    half = cos.shape[-1]
    c = cos[:, :, None, :]
    s = sin[:, :, None, :]
    x1 = x[..., :half].astype(jnp.float32)
    x2 = x[..., half:2 * half].astype(jnp.float32)
    rot = jnp.concatenate([x1 * c - x2 * s, x2 * c + x1 * s], axis=-1).astype(x.dtype)
    return jnp.concatenate([rot, x[..., 2 * half:]], axis=-1)


def to_blocks(a):
    b, s = a.shape[:2]
    a = a.reshape((b, s // Q_BLOCK, Q_BLOCK) + a.shape[2:])
    return jnp.moveaxis(a, 1, 0)


def from_blocks(a):
    a = jnp.moveaxis(a, 0, 1)
    return a.reshape((a.shape[0], a.shape[1] * a.shape[2]) + a.shape[3:])


def dsa_attention(q, k, v, q_idx, k_idx, w_idx, top_k):
    seq = q.shape[1]
    key_pos = jnp.arange(seq, dtype=jnp.int32)
    k_idx_f = k_idx.astype(jnp.float32)
    scale = HEAD_DIM ** -0.5
    idx_scale = IDX_DIM ** -0.5
    head_w_scale = IDX_HEADS ** -0.5

    def block(args):
        qb, qib, wb, tb = args
        logits = jnp.einsum('bqhd,bsd->bqhs', qib.astype(jnp.float32), k_idx_f) * idx_scale
        score = jnp.einsum('bqhs,bqh->bqs', jax.nn.relu(logits),
                           wb.astype(jnp.float32) * head_w_scale)
        causal = key_pos[None, :] <= tb[:, None]
        score = jnp.where(causal[None], score, -jnp.inf)
        _, sel = lax.top_k(score, top_k)
        ks = jax.vmap(lambda kk, ii: kk[ii])(k, sel)
        vs = jax.vmap(lambda vv, ii: vv[ii])(v, sel)
        s = jnp.einsum('bqhd,bqkhd->bqhk', qb.astype(jnp.float32),
                       ks.astype(jnp.float32)) * scale
        valid = sel <= tb[None, :, None]
        s = jnp.where(valid[:, :, None, :], s, -jnp.inf)
        p = jax.nn.softmax(s, axis=-1)
        return jnp.einsum('bqhk,bqkhd->bqhd', p, vs.astype(jnp.float32)).astype(qb.dtype)

    pos_blocks = key_pos.reshape(seq // Q_BLOCK, Q_BLOCK)
    out = lax.map(block, (to_blocks(q), to_blocks(q_idx), to_blocks(w_idx), pos_blocks))
    return from_blocks(out)


def dilated_attention(q, k, v):
    seq = q.shape[1]
    hpg = B_HEADS_PER_GROUP
    scale = HEAD_DIM ** -0.5
    k_groups = [k[:, :, g * hpg:(g + 1) * hpg] for g in range(len(B_GROUPS))]
    v_groups = [v[:, :, g * hpg:(g + 1) * hpg] for g in range(len(B_GROUPS))]

    def block(args):
        qb, tb = args
        outs, lses = [], []
        for g, (window, dil) in enumerate(B_GROUPS):
            n_keys = window // dil + 1
            idx = tb[:, None] - dil * jnp.arange(n_keys, dtype=jnp.int32)[None, :]
            valid = idx >= 0
            idx_c = jnp.maximum(idx, 0)
            kg = jnp.take(k_groups[g], idx_c, axis=1).astype(jnp.float32)
            vg = jnp.take(v_groups[g], idx_c, axis=1).astype(jnp.float32)
            qg = qb[:, :, g * hpg:(g + 1) * hpg].astype(jnp.float32)
            s = jnp.einsum('bqhd,bqjhd->bqhj', qg, kg) * scale
            s = jnp.where(valid[None, :, None, :], s, -jnp.inf)
            lse = jax.nn.logsumexp(s, axis=-1)
            p = jnp.exp(s - lse[..., None])
            outs.append(jnp.einsum('bqhj,bqjhd->bqhd', p, vg))
            lses.append(lse)
        wgt = jax.nn.softmax(jnp.stack(lses, axis=0), axis=0)
        o = jnp.sum(wgt[..., None] * jnp.stack(outs, axis=0), axis=0)
        return o.astype(qb.dtype)

    pos_blocks = jnp.arange(seq, dtype=jnp.int32).reshape(seq // Q_BLOCK, Q_BLOCK)
    return from_blocks(lax.map(block, (to_blocks(q), pos_blocks)))


def gla_chunked(q, k, v, log_alpha):
    bsz, seq, nh, dk = q.shape
    dv = v.shape[-1]
    csz = min(C_CHUNK, seq)
    nc = seq // csz
    f32 = jnp.float32
    q = q.astype(f32).reshape(bsz, nc, csz, nh, dk) * dk ** -0.5
    k = k.astype(f32).reshape(bsz, nc, csz, nh, dk)
    v = v.astype(f32).reshape(bsz, nc, csz, nh, dv)
    b = jnp.cumsum(log_alpha.astype(f32).reshape(bsz, nc, csz, nh, dk), axis=2)
    b_last = b[:, :, -1]
    q_dec = q * jnp.exp(b)
    k_dec = k * jnp.exp(-b)
    attn = jnp.einsum('bnthd,bnshd->bnhts', q_dec, k_dec)
    attn = jnp.where(jnp.tril(jnp.ones((csz, csz), dtype=bool)), attn, 0.0)
    o_intra = jnp.einsum('bnhts,bnshv->bnthv', attn, v)
    k_to_end = k * jnp.exp(b_last[:, :, None] - b)
    kv = jnp.einsum('bnshk,bnshv->bnhkv', k_to_end, v)
    decay = jnp.exp(b_last)

    def step(state, inp):
        kv_n, dec_n = inp
        return dec_n[..., None] * state + kv_n, state

    init = jnp.zeros((bsz, nh, dk, dv), f32)
    _, s_prev = lax.scan(step, init, (jnp.moveaxis(kv, 1, 0), jnp.moveaxis(decay, 1, 0)))
    s_prev = jnp.moveaxis(s_prev, 0, 1)
    o_inter = jnp.einsum('bnthk,bnhkv->bnthv', q_dec, s_prev)
    return (o_intra + o_inter).reshape(bsz, seq, nh, dv)


def hybrid_mixer(u, cos, sin, top_k, w_in, a_kv_norm_g, a_w_kv_up, c_w_alpha_up, c_b_alpha,
                 c_norm_g, w_branch_a, w_branch_b, w_branch_c, w_gate, b_gate, w_out):
    bsz, seq, d = u.shape
    split_points = np.cumsum(IN_SIZES)[:-1].tolist()
    (a_q, a_ckv, i_q, i_k, i_w, b_qkv, c_q, c_k, c_v, c_g, c_a) = jnp.split(
        u @ w_in, split_points, axis=-1)

    q_a = partial_rope(a_q.reshape(bsz, seq, A_HEADS, HEAD_DIM), cos, sin)
    kv_a = (rms_norm(a_ckv, a_kv_norm_g) @ a_w_kv_up).reshape(bsz, seq, A_HEADS, 2, HEAD_DIM)
    k_a = partial_rope(kv_a[..., 0, :], cos, sin)
    v_a = kv_a[..., 1, :]
    q_i = partial_rope(i_q.reshape(bsz, seq, IDX_HEADS, IDX_DIM), cos, sin)
    k_i = partial_rope(i_k[:, :, None, :], cos, sin)[:, :, 0]
    y_a = dsa_attention(q_a, k_a, v_a, q_i, k_i, i_w, top_k).reshape(bsz, seq, A_HEADS * HEAD_DIM)

    qkv_b = b_qkv.reshape(bsz, seq, 3, B_HEADS, HEAD_DIM)
    q_b = partial_rope(qkv_b[:, :, 0], cos, sin)
    k_b = partial_rope(qkv_b[:, :, 1], cos, sin)
    y_b = dilated_attention(q_b, k_b, qkv_b[:, :, 2]).reshape(bsz, seq, B_HEADS_PER_GROUP * HEAD_DIM)

    log_alpha = jax.nn.log_sigmoid((c_a @ c_w_alpha_up + c_b_alpha).astype(jnp.float32)) / C_GATE_TAU
    o_c = gla_chunked(c_q.reshape(bsz, seq, C_HEADS, C_HEAD_K),
                      c_k.reshape(bsz, seq, C_HEADS, C_HEAD_K),
                      c_v.reshape(bsz, seq, C_HEADS, C_HEAD_V),
                      log_alpha.reshape(bsz, seq, C_HEADS, C_HEAD_K))
    o_c = rms_norm(o_c.astype(u.dtype), c_norm_g) * jax.nn.silu(c_g.reshape(bsz, seq, C_HEADS, C_HEAD_V))
    y_c = o_c.reshape(bsz, seq, C_HEADS * C_HEAD_V)

    gates = jax.nn.sigmoid(u @ w_gate + b_gate).reshape(bsz, seq, N_BRANCH, d)
    merged = (gates[:, :, 0] * (y_a @ w_branch_a)
              + gates[:, :, 1] * (y_b @ w_branch_b)
              + gates[:, :, 2] * (y_c @ w_branch_c))
    return merged @ w_out


def setup_inputs(seed: int = 0) -> dict:
    key = jax.random.key(seed)
    ks = jax.random.split(key, 24)

    def nrm(i, shape, fan_in):
        return jax.random.normal(ks[i], shape, jnp.float32) * fan_in ** -0.5

    def gain(i, n):
        return 1.0 + 0.02 * jax.random.normal(ks[i], (DEPTH, n), jnp.float32)

    x = jax.random.normal(ks[0], (BATCH, SEQ, D_MODEL), jnp.float32)
    positions = (jax.random.randint(ks[1], (BATCH, 1), 0, 4096, dtype=jnp.int32)
                 + jnp.arange(SEQ, dtype=jnp.int32)[None, :])
    return {
        'x': x,
        'positions': positions,
        'ffn1_pre_g': gain(2, D_MODEL),
        'ffn1_w_in': nrm(3, (DEPTH, D_MODEL, 2 * D_FF), D_MODEL),
        'ffn1_w_out': nrm(4, (DEPTH, D_FF, D_MODEL), D_FF),
        'ffn1_post_g': gain(5, D_MODEL),
        'mix_pre_g': gain(6, D_MODEL),
        'w_in': nrm(7, (DEPTH, D_MODEL, D_IN), D_MODEL),
        'a_kv_norm_g': gain(8, A_KV_RANK),
        'a_w_kv_up': nrm(9, (DEPTH, A_KV_RANK, 2 * A_HEADS * HEAD_DIM), A_KV_RANK),
        'c_w_alpha_up': nrm(10, (DEPTH, C_ALPHA_RANK, C_HEADS * C_HEAD_K), C_ALPHA_RANK),
        'c_b_alpha': 0.1 * jax.random.normal(ks[11], (DEPTH, C_HEADS * C_HEAD_K), jnp.float32),
        'c_norm_g': gain(12, C_HEAD_V),
        'w_branch_a': nrm(13, (DEPTH, A_HEADS * HEAD_DIM, D_MODEL), A_HEADS * HEAD_DIM),
        'w_branch_b': nrm(14, (DEPTH, B_HEADS_PER_GROUP * HEAD_DIM, D_MODEL), B_HEADS_PER_GROUP * HEAD_DIM),
        'w_branch_c': nrm(15, (DEPTH, C_HEADS * C_HEAD_V, D_MODEL), C_HEADS * C_HEAD_V),
        'w_gate': nrm(16, (DEPTH, D_MODEL, N_BRANCH * D_MODEL), D_MODEL),
        'b_gate': 0.02 * jax.random.normal(ks[17], (DEPTH, N_BRANCH * D_MODEL), jnp.float32),
        'w_out': nrm(18, (DEPTH, D_MODEL, D_MODEL), D_MODEL),
        'mix_post_g': gain(19, D_MODEL),
        'ffn2_pre_g': gain(20, D_MODEL),
        'ffn2_w_in': nrm(21, (DEPTH, D_MODEL, 2 * D_FF), D_MODEL),
        'ffn2_w_out': nrm(22, (DEPTH, D_FF, D_MODEL), D_FF),
        'ffn2_post_g': gain(23, D_MODEL),
    }


def reference(x, positions, ffn1_pre_g, ffn1_w_in, ffn1_w_out, ffn1_post_g, mix_pre_g, w_in,
              a_kv_norm_g, a_w_kv_up, c_w_alpha_up, c_b_alpha, c_norm_g, w_branch_a, w_branch_b,
              w_branch_c, w_gate, b_gate, w_out, mix_post_g, ffn2_pre_g, ffn2_w_in, ffn2_w_out,
              ffn2_post_g):
    top_k = min(TOPK_MAX, x.shape[1] // 4)
    cos, sin = rope_tables(positions, ROT_DIM)
    h = x
    for l in range(DEPTH):
        f1 = swiglu(rms_norm(h, ffn1_pre_g[l]), ffn1_w_in[l], ffn1_w_out[l])
        h = h + MACARON_WEIGHT * rms_norm(f1, ffn1_post_g[l])
        m = hybrid_mixer(rms_norm(h, mix_pre_g[l]), cos, sin, top_k, w_in[l], a_kv_norm_g[l],
                         a_w_kv_up[l], c_w_alpha_up[l], c_b_alpha[l], c_norm_g[l], w_branch_a[l],
                         w_branch_b[l], w_branch_c[l], w_gate[l], b_gate[l], w_out[l])
        h = h + rms_norm(m, mix_post_g[l])
        f2 = swiglu(rms_norm(h, ffn2_pre_g[l]), ffn2_w_in[l], ffn2_w_out[l])
        h = h + MACARON_WEIGHT * rms_norm(f2, ffn2_post_g[l])
    return h
```

```python
import functools

import jax
import jax.numpy as jnp
from jax import lax
from jax.experimental import pallas as pl
from jax.experimental.pallas import tpu as pltpu

F32 = jnp.float32
BF16 = jnp.bfloat16

HEAD_DIM = 64
ROT_DIM = HEAD_DIM // 4
ROPE_THETA = 500000.0
RMS_EPS = 1e-6
MACARON_WEIGHT = 0.5

A_HEADS = 8
A_KV_RANK = 128
IDX_HEADS = 8
IDX_DIM = 64
TOPK_MAX = 256

B_GROUPS = ((128, 1), (512, 4), (2048, 16))
B_HEADS_PER_GROUP = 4
B_HEADS = B_HEADS_PER_GROUP * len(B_GROUPS)

C_HEADS = 4
C_HEAD_K = 64
C_HEAD_V = 128
C_ALPHA_RANK = 16
C_GATE_TAU = 16.0
C_CHUNK = 64

LANES = 128
Q_BLOCK = 128
DSA_KEY_CHUNK = 512
BISECT_MAX_ITERS = 48
MASKED_LOGIT = -1e30
VMEM_LIMIT = 60 * 1024 * 1024

SMALL_IK = 0
SMALL_IW = IDX_DIM
SMALL_CA = IDX_DIM + IDX_HEADS


def _dot(a, b):
    return jnp.dot(a, b, preferred_element_type=F32)


def _dot_nt(a, b):
    return lax.dot_general(a, b, (((1,), (1,)), ((), ())), preferred_element_type=F32)


def _dot_tn(a, b):
    return lax.dot_general(a, b, (((0,), (0,)), ((), ())), preferred_element_type=F32)


def _rms(x, g):
    return x * lax.rsqrt(jnp.mean(x * x, axis=-1, keepdims=True) + RMS_EPS) * g


def _silu(x):
    return x * jax.nn.sigmoid(x)


def _params(n_grid):
    return pltpu.CompilerParams(dimension_semantics=("arbitrary",) * n_grid,
                                vmem_limit_bytes=VMEM_LIMIT)


def _resident(shape):
    zeros = (0,) * len(shape)
    return pl.BlockSpec(shape, lambda *_: zeros, pipeline_mode=pl.Buffered(1))


def _half_mask(shape, upper):
    lane = lax.broadcasted_iota(jnp.int32, shape, len(shape) - 1) % LANES
    return (lane >= HEAD_DIM) if upper else (lane < HEAD_DIM)


def _ffn_body(h_ref, pre_g_ref, w_in_ref, w_out_ref, post_g_ref, o_ref, *, d_ff, tf):
    x = h_ref[...]
    xn = _rms(x, pre_g_ref[...]).astype(BF16)
    acc = jnp.zeros(x.shape, F32)
    for j in range(d_ff // tf):
        gate = _dot(xn, w_in_ref[:, j * tf:(j + 1) * tf])
        up = _dot(xn, w_in_ref[:, d_ff + j * tf:d_ff + (j + 1) * tf])
        act = (_silu(gate) * up).astype(BF16)
        acc = acc + _dot(act, w_out_ref[j * tf:(j + 1) * tf, :])
    o_ref[...] = x + MACARON_WEIGHT * _rms(acc, post_g_ref[...])


def _ffn(h, pre_g, w_in, w_out, post_g, *, tm=512):
    t, d = h.shape
    d_ff = w_out.shape[0]
    tf = d_ff // 2 if (d_ff // 2) % LANES == 0 else d_ff
    row = lambda i: (i, 0)
    return pl.pallas_call(
        functools.partial(_ffn_body, d_ff=d_ff, tf=tf),
        grid=(t // tm,),
        in_specs=[pl.BlockSpec((tm, d), row), _resident((1, d)), _resident(w_in.shape),
                  _resident(w_out.shape), _resident((1, d))],
        out_specs=pl.BlockSpec((tm, d), row),
        out_shape=jax.ShapeDtypeStruct((t, d), F32),
        compiler_params=_params(1),
        name="ffn",
    )(h, pre_g, w_in, w_out, post_g)


_IN_COLS = (("a_q", 512), ("i_q", 512), ("b_q", 768), ("b_k", 768), ("b_v", 768),
            ("c_q", 256), ("c_k", 256), ("c_v", 512), ("c_g", 512), ("ckv", 128), ("small", 128))


def _in_col_offsets():
    offs, o = {}, 0
    for name, n in _IN_COLS:
        offs[name] = (o, o + n)
        o += n
    return offs, o


def _rope(x, cos_t, sinp_t, sinm_t):
    outs = []
    for i in range(x.shape[1] // LANES):
        xs = x[:, i * LANES:(i + 1) * LANES]
        outs.append(xs * cos_t + pltpu.roll(xs, 8, 1) * sinp_t + pltpu.roll(xs, LANES - 8, 1) * sinm_t)
    return outs[0] if len(outs) == 1 else jnp.concatenate(outs, axis=1)


def _log_sigmoid(x):
    return jnp.minimum(x, 0.0) - jnp.log1p(jnp.exp(-jnp.abs(x)))


def _mixer_in_body(h_ref, g_ref, w_ref, cos_ref, sinp_ref, sinm_ref, kvg_ref, wk_ref, wv_ref,
                   wal_ref, bal_ref,
                   qa_ref, ka_ref, va_ref, qi_ref, ki_ref, wi_ref, bq_ref, bk_ref, bv_ref,
                   cq_ref, ck_ref, cv_ref, cg_ref, la_ref):
    offs, _ = _in_col_offsets()
    u = _rms(h_ref[...], g_ref[...]).astype(BF16)
    cos_t, sinp_t, sinm_t = cos_ref[...], sinp_ref[...], sinm_ref[...]

    def proj(name):
        lo, hi = offs[name]
        return _dot(u, w_ref[:, lo:hi])

    rope = lambda x: _rope(x, cos_t, sinp_t, sinm_t)
    q_scale = HEAD_DIM ** -0.5
    qa_ref[...] = (rope(proj("a_q")) * q_scale).astype(BF16)
    qi_ref[...] = (rope(proj("i_q")) * (IDX_DIM ** -0.5)).astype(BF16)
    bq_ref[...] = (rope(proj("b_q")) * q_scale).astype(BF16)
    bk_ref[...] = rope(proj("b_k")).astype(BF16)
    bv_ref[...] = proj("b_v").astype(BF16)
    cq_ref[...] = proj("c_q") * (C_HEAD_K ** -0.5)
    ck_ref[...] = proj("c_k")
    cv_ref[...] = proj("c_v")
    cg_ref[...] = proj("c_g")

    ckv = _rms(proj("ckv"), kvg_ref[...]).astype(BF16)
    ka_ref[...] = rope(_dot(ckv, wk_ref[...])).astype(BF16)
    va_ref[...] = _dot(ckv, wv_ref[...]).astype(BF16)

    small = proj("small")
    small_r = rope(small)
    ki_ref[...] = jnp.where(_half_mask(small.shape, False), small_r,
                            pltpu.roll(small_r, HEAD_DIM, 1)).astype(BF16)
    wi_ref[...] = small[:, SMALL_IW:SMALL_IW + IDX_HEADS] * (IDX_HEADS ** -0.5)
    la = _dot(small.astype(BF16), wal_ref[...]) + bal_ref[...]
    la_ref[...] = _log_sigmoid(la) * (1.0 / C_GATE_TAU)


def _mixer_in(h, g, w_big, cos_t, sinp_t, sinm_t, kv_g, w_kup, w_vup, w_alpha, b_alpha, *, tm=512):
    t, d = h.shape
    row = lambda i: (i, 0)
    out_defs = [("q_a", 512, BF16), ("k_a", 512, BF16), ("v_a", 512, BF16), ("q_i", 512, BF16),
                ("k_i", 128, BF16), ("w_i", IDX_HEADS, F32), ("b_q", 768, BF16), ("b_k", 768, BF16),
                ("b_v", 768, BF16), ("c_q", 256, F32), ("c_k", 256, F32), ("c_v", 512, F32),
                ("c_g", 512, F32), ("la", 256, F32)]
    outs = pl.pallas_call(
        _mixer_in_body,
        grid=(t // tm,),
        in_specs=[pl.BlockSpec((tm, d), row), _resident((1, d)), _resident(w_big.shape),
                  pl.BlockSpec((tm, LANES), row), pl.BlockSpec((tm, LANES), row),
                  pl.BlockSpec((tm, LANES), row), _resident(kv_g.shape), _resident(w_kup.shape),
                  _resident(w_vup.shape), _resident(w_alpha.shape), _resident(b_alpha.shape)],
        out_specs=[pl.BlockSpec((tm, n), row) for _, n, _ in out_defs],
        out_shape=[jax.ShapeDtypeStruct((t, n), dt) for _, n, dt in out_defs],
        compiler_params=_params(1),
        name="mixer_in",
    )(h, g, w_big, cos_t, sinp_t, sinm_t, kv_g, w_kup, w_vup, w_alpha, b_alpha)
    return {name: o for (name, _, _), o in zip(out_defs, outs)}


def _dsa_body(qi_ref, wi_ref, qa_ref, ki_ref, ka_ref, va_ref, o_ref,
              score_ref, wb_ref, m_ref, l_ref, acc_ref, *, seq, top_k):
    kc = DSA_KEY_CHUNK
    qb = Q_BLOCK
    i = pl.program_id(1)
    n_chunks = (i * qb + qb + kc - 1) // kc
    t_pos = i * qb + lax.broadcasted_iota(jnp.int32, (qb, 1), 0)
    lane_pos = lax.broadcasted_iota(jnp.int32, (qb, kc), 1)

    qi = qi_ref[0]
    q_rows = []
    for h in range(IDX_HEADS):
        pair = qi[:, (h // 2) * LANES:(h // 2 + 1) * LANES]
        q_rows.append(jnp.where(_half_mask(pair.shape, h % 2 == 1), pair, jnp.zeros_like(pair)))
    q8 = jnp.concatenate(q_rows, axis=0)
    wi = wi_ref[0]
    for h in range(IDX_HEADS):
        wb_ref[h] = jnp.broadcast_to(wi[:, h:h + 1], (qb, kc))

    def score_step(c, carry):
        mx, mn = carry
        k_c = ki_ref[0, pl.ds(pl.multiple_of(c * kc, kc), kc), :]
        logits = _dot_nt(q8, k_c)
        sc = jnp.zeros((qb, kc), F32)
        for h in range(IDX_HEADS):
            sc = sc + jnp.maximum(logits[h * qb:(h + 1) * qb], 0.0) * wb_ref[h]
        causal = (c * kc + lane_pos) <= t_pos
        score_ref[c] = jnp.where(causal, sc, -jnp.inf)
        mx = jnp.maximum(mx, jnp.max(jnp.where(causal, sc, -jnp.inf), axis=1, keepdims=True))
        mn = jnp.minimum(mn, jnp.min(jnp.where(causal, sc, jnp.inf), axis=1, keepdims=True))
        return mx, mn

    row_max, row_min = lax.fori_loop(
        0, n_chunks, score_step,
        (jnp.full((qb, 1), -jnp.inf, F32), jnp.full((qb, 1), jnp.inf, F32)))

    def count(pred):
        def step(c, acc):
            hit = jnp.where(pred(score_ref[c], c), 1.0, 0.0)
            for j in range(kc // LANES):
                acc = acc + hit[:, j * LANES:(j + 1) * LANES]
            return acc
        acc = lax.fori_loop(0, n_chunks, step, jnp.zeros((qb, LANES), F32))
        return jnp.sum(acc, axis=1, keepdims=True)

    k_f = float(top_k)
    n_causal = (t_pos + 1).astype(F32)
    few = n_causal <= k_f
    lo0 = row_min
    hi0 = row_max + jnp.abs(row_max) + 1.0

    def bisect_cond(st):
        it, _, _, _, _, pending = st
        return jnp.logical_and(it < BISECT_MAX_ITERS, pending > 0.5)

    def bisect_step(st):
        it, lo, hi, c_lo, c_hi, _ = st
        mid = lo + (hi - lo) * 0.5
        cnt = count(lambda x, c: x >= mid)
        ge = cnt >= k_f
        lo_n = jnp.where(ge, mid, lo)
        hi_n = jnp.where(ge, hi, mid)
        c_lo_n = jnp.where(ge, cnt, c_lo)
        c_hi_n = jnp.where(ge, c_hi, cnt)
        stalled = jnp.logical_or(mid <= lo, mid >= hi)
        done = few | (c_lo_n == k_f) | stalled
        pending = jnp.max(jnp.where(done, 0.0, 1.0))
        return it + 1, lo_n, hi_n, c_lo_n, c_hi_n, pending

    pending0 = jnp.max(jnp.where(few, 0.0, 1.0))
    _, lo, hi, c_lo, c_hi, _ = lax.while_loop(
        bisect_cond, bisect_step,
        (jnp.int32(0), lo0, hi0, n_causal, jnp.zeros((qb, 1), F32), pending0))

    tied = jnp.logical_and(jnp.logical_not(few), c_lo > k_f)
    need = k_f - c_hi

    def tie_search(_):
        def step(_, st):
            j_lo, j_hi = st
            mid = jnp.floor((j_lo + j_hi) * 0.5)
            cnt = count(lambda x, c: (x >= lo) & (x < hi)
                        & ((c * kc + lane_pos).astype(F32) <= mid))
            ok = cnt >= need
            return jnp.where(ok, j_lo, mid), jnp.where(ok, mid, j_hi)
        n_steps = max(1, (seq - 1).bit_length()) + 1
        _, j_hi = lax.fori_loop(0, n_steps, step,
                                (jnp.full((qb, 1), -1.0, F32), jnp.full((qb, 1), float(seq - 1), F32)))
        return j_hi

    any_tied = jnp.max(jnp.where(tied, 1.0, 0.0)) > 0.5
    j_cut = lax.cond(any_tied, tie_search, lambda _: jnp.full((qb, 1), float(seq), F32), 0)
    j_cut = jnp.where(tied, j_cut, float(seq))
    thr_hi = jnp.where(tied, hi, lo)

    qa = qa_ref[0]
    q_heads = []
    for h in range(A_HEADS):
        pair = qa[:, (h // 2) * LANES:(h // 2 + 1) * LANES]
        q_heads.append(jnp.where(_half_mask(pair.shape, h % 2 == 1), pair, jnp.zeros_like(pair)))
    m_ref[...] = jnp.full(m_ref.shape, MASKED_LOGIT, F32)
    l_ref[...] = jnp.zeros(l_ref.shape, F32)
    acc_ref[...] = jnp.zeros(acc_ref.shape, F32)

    def attn_step(c, carry):
        x = score_ref[c]
        k_idx = (c * kc + lane_pos).astype(F32)
        sel = (x >= thr_hi) | ((x >= lo) & (k_idx <= j_cut))
        off = pl.multiple_of(c * kc, kc)
        for h in range(A_HEADS):
            p_lo = (h // 2) * LANES
            k_c = ka_ref[0, pl.ds(off, kc), p_lo:p_lo + LANES]
            v_c = va_ref[0, pl.ds(off, kc), p_lo:p_lo + LANES]
            s = jnp.where(sel, _dot_nt(q_heads[h], k_c), MASKED_LOGIT)
            m_old = m_ref[h]
            m_new = jnp.maximum(m_old, jnp.max(s, axis=1, keepdims=True))
            alpha = jnp.exp(m_old - m_new)
            p = jnp.exp(s - m_new)
            l_ref[h] = alpha * l_ref[h] + jnp.sum(p, axis=1, keepdims=True)
            acc_ref[h] = alpha * acc_ref[h] + _dot(p.astype(BF16), v_c)
            m_ref[h] = m_new
        return carry

    lax.fori_loop(0, n_chunks, attn_step, 0)

    outs = []
    for pr in range(A_HEADS // 2):
        o_even = acc_ref[2 * pr] / l_ref[2 * pr]
        o_odd = acc_ref[2 * pr + 1] / l_ref[2 * pr + 1]
        outs.append(jnp.where(_half_mask(o_even.shape, False), o_even, o_odd))
    o_ref[0] = jnp.concatenate(outs, axis=1).astype(BF16)


def _dsa(q_i, w_i, q_a, k_i, k_a, v_a, *, top_k):
    bsz, seq, _ = q_a.shape
    qb, kc = Q_BLOCK, DSA_KEY_CHUNK
    blk = lambda b, i: (b, i, 0)
    full = lambda b, i: (b, 0, 0)
    whole = lambda n: pl.BlockSpec((1, seq, n), full, pipeline_mode=pl.Buffered(1))
    return pl.pallas_call(
        functools.partial(_dsa_body, seq=seq, top_k=top_k),
        grid=(bsz, seq // qb),
        in_specs=[pl.BlockSpec((1, qb, 512), blk), pl.BlockSpec((1, qb, IDX_HEADS), blk),
                  pl.BlockSpec((1, qb, 512), blk), whole(LANES), whole(512), whole(512)],
        out_specs=pl.BlockSpec((1, qb, 512), blk),
        out_shape=jax.ShapeDtypeStruct((bsz, seq, 512), BF16),
        scratch_shapes=[pltpu.VMEM((seq // kc, qb, kc), F32),
                        pltpu.VMEM((IDX_HEADS, qb, kc), F32),
                        pltpu.VMEM((A_HEADS, qb, 1), F32),
                        pltpu.VMEM((A_HEADS, qb, 1), F32),
                        pltpu.VMEM((A_HEADS, qb, LANES), F32)],
        compiler_params=_params(2),
        name="dsa",
    )(q_i, w_i, q_a, k_i, k_a, v_a)


def _dilated_body(q_ref, kp_ref, k_ref, vp_ref, v_ref, o_ref, lse_ref, *, tq):
    qb = Q_BLOCK
    n = pl.program_id(2)
    row = lax.broadcasted_iota(jnp.int32, (qb, 2 * qb), 0)
    col = lax.broadcasted_iota(jnp.int32, (qb, 2 * qb), 1)
    rel = row + qb - col
    band = (rel >= 0) & (rel <= qb)
    n_pairs = B_HEADS_PER_GROUP // 2
    for s in range(tq // qb):
        q = q_ref[0, s * qb:(s + 1) * qb, :]
        if s == 0:
            k_prev, v_prev = kp_ref[0], vp_ref[0]
            valid = band & ((col >= qb) | (n > 0))
        else:
            k_prev = k_ref[0, (s - 1) * qb:s * qb, :]
            v_prev = v_ref[0, (s - 1) * qb:s * qb, :]
            valid = band
        kk = jnp.concatenate([k_prev, k_ref[0, s * qb:(s + 1) * qb, :]], axis=0)
        vv = jnp.concatenate([v_prev, v_ref[0, s * qb:(s + 1) * qb, :]], axis=0)
        o_parts, lse_parts = [], []
        for pr in range(n_pairs):
            q_pair = q[:, pr * LANES:(pr + 1) * LANES]
            k_pair = kk[:, pr * LANES:(pr + 1) * LANES]
            v_pair = vv[:, pr * LANES:(pr + 1) * LANES]
            o_h, lse_h = [], []
            for odd in (False, True):
                q_h = jnp.where(_half_mask(q_pair.shape, odd), q_pair, jnp.zeros_like(q_pair))
                sc = jnp.where(valid, _dot_nt(q_h, k_pair), -jnp.inf)
                m = jnp.max(sc, axis=1, keepdims=True)
                p = jnp.exp(sc - m)
                l = jnp.sum(p, axis=1, keepdims=True)
                o_h.append(_dot(p.astype(BF16), v_pair) / l)
                lse_h.append(jnp.broadcast_to(m + jnp.log(l), (qb, LANES)))
            lower = _half_mask((qb, LANES), False)
            o_parts.append(jnp.where(lower, o_h[0], o_h[1]))
            lse_parts.append(jnp.where(lower, lse_h[0], lse_h[1]))
        o_ref[0, s * qb:(s + 1) * qb, :] = jnp.concatenate(o_parts, axis=1)
        lse_ref[0, s * qb:(s + 1) * qb, :] = jnp.concatenate(lse_parts, axis=1)


def _dilated_group(b_q, b_k, b_v, g, dil):
    bsz, seq, width = b_q.shape
    sub = seq // dil
    qb = Q_BLOCK
    tq = min(512, sub)
    gw = B_HEADS_PER_GROUP * HEAD_DIM
    n_grp = width // gw
    view = lambda a: a.reshape(bsz, sub, dil * width)
    cur = lambda b, r, n: (b, n, r * n_grp + g)
    prev = lambda b, r, n: (b, jnp.maximum(n * (tq // qb) - 1, 0), r * n_grp + g)
    out = lambda b, r, n: (b, n, r)
    o, lse = pl.pallas_call(
        functools.partial(_dilated_body, tq=tq),
        grid=(bsz, dil, sub // tq),
        in_specs=[pl.BlockSpec((1, tq, gw), cur),
                  pl.BlockSpec((1, qb, gw), prev), pl.BlockSpec((1, tq, gw), cur),
                  pl.BlockSpec((1, qb, gw), prev), pl.BlockSpec((1, tq, gw), cur)],
        out_specs=[pl.BlockSpec((1, tq, gw), out), pl.BlockSpec((1, tq, gw), out)],
        out_shape=[jax.ShapeDtypeStruct((bsz, sub, dil * gw), F32)] * 2,
        compiler_params=_params(3),
        name=f"dilated_d{dil}",
    )(view(b_q), view(b_k), view(b_k), view(b_v), view(b_v))
    return o.reshape(bsz * seq, gw), lse.reshape(bsz * seq, gw)


def _gla_body(q_ref, k_ref, v_ref, g_ref, la_ref, ng_ref, o_ref, st_ref, *, n_sub):
    cs = C_CHUNK

    @pl.when(pl.program_id(1) == 0)
    def _():
        st_ref[...] = jnp.zeros(st_ref.shape, F32)

    r_i = lax.broadcasted_iota(jnp.int32, (cs, cs), 0)
    c_i = lax.broadcasted_iota(jnp.int32, (cs, cs), 1)
    tril = r_i >= c_i
    tri_ones = jnp.where(tril, 1.0, 0.0).astype(BF16)
    ng = ng_ref[...]
    for c in range(n_sub):
        rows = slice(c * cs, (c + 1) * cs)
        la = la_ref[0, rows, :]
        la_hi = la.astype(BF16)
        la_lo = (la - la_hi.astype(F32)).astype(BF16)
        b = _dot(tri_ones, la_hi) + _dot(tri_ones, la_lo)
        b_last = b[cs - 1:cs, :]
        q = q_ref[0, rows, :]
        k = k_ref[0, rows, :]
        q_dec = q * jnp.exp(b)
        k_dec = (k * jnp.exp(-b)).astype(BF16)
        k_end = (k * jnp.exp(b_last - b)).astype(BF16)
        decay = jnp.exp(b_last)
        o_heads = []
        for h in range(C_HEADS):
            p_lo = (h // 2) * LANES
            q_pair = q_dec[:, p_lo:p_lo + LANES]
            q_h = jnp.where(_half_mask(q_pair.shape, h % 2 == 1), q_pair, 0.0).astype(BF16)
            v_h = v_ref[0, rows, h * C_HEAD_V:(h + 1) * C_HEAD_V].astype(BF16)
            attn = jnp.where(tril, _dot_nt(q_h, k_dec[:, p_lo:p_lo + LANES]), 0.0)
            st = st_ref[h]
            o_h = _dot(attn.astype(BF16), v_h) + _dot_nt(q_h, st.astype(BF16))
            st_ref[h] = st * decay[:, p_lo:p_lo + LANES] + _dot_tn(v_h, k_end[:, p_lo:p_lo + LANES])
            gate = g_ref[0, rows, h * C_HEAD_V:(h + 1) * C_HEAD_V]
            o_heads.append(_rms(o_h, ng) * _silu(gate))
        o_ref[0, rows, :] = jnp.concatenate(o_heads, axis=1).astype(BF16)


def _gla(c_q, c_k, c_v, c_g, la, norm_g, *, n_sub=8):
    bsz, seq, _ = c_q.shape
    rows = n_sub * C_CHUNK
    blk = lambda b, n: (b, n, 0)
    kw, vw = C_HEADS * C_HEAD_K, C_HEADS * C_HEAD_V
    return pl.pallas_call(
        functools.partial(_gla_body, n_sub=n_sub),
        grid=(bsz, seq // rows),
        in_specs=[pl.BlockSpec((1, rows, kw), blk), pl.BlockSpec((1, rows, kw), blk),
                  pl.BlockSpec((1, rows, vw), blk), pl.BlockSpec((1, rows, vw), blk),
                  pl.BlockSpec((1, rows, kw), blk), _resident(norm_g.shape)],
        out_specs=pl.BlockSpec((1, rows, vw), blk),
        out_shape=jax.ShapeDtypeStruct((bsz, seq, vw), BF16),
        scratch_shapes=[pltpu.VMEM((C_HEADS, C_HEAD_V, LANES), F32)],
        compiler_params=_params(2),
        name="gla",
    )(c_q, c_k, c_v, c_g, la, norm_g)


def _mixer_out_body(h_ref, ya_ref, ob0_ref, ob1_ref, ob2_ref, ls0_ref, ls1_ref, ls2_ref, yc_ref,
                    pre_g_ref, wg_ref, bg_ref, wa_ref, wb_ref, wc_ref, wo_ref, post_g_ref, o_ref):
    x = h_ref[...]
    d = x.shape[1]
    u = _rms(x, pre_g_ref[...]).astype(BF16)

    ls = [ls0_ref[...], ls1_ref[...], ls2_ref[...]]
    ob = [ob0_ref[...], ob1_ref[...], ob2_ref[...]]
    mx = jnp.maximum(jnp.maximum(ls[0], ls[1]), ls[2])
    ew = [jnp.exp(v - mx) for v in ls]
    y_b = ((ew[0] * ob[0] + ew[1] * ob[1] + ew[2] * ob[2]) / (ew[0] + ew[1] + ew[2])).astype(BF16)

    branches = ((ya_ref[...], wa_ref), (y_b, wb_ref), (yc_ref[...], wc_ref))
    merged = jnp.zeros(x.shape, F32)
    for i, (y, w_ref) in enumerate(branches):
        gate = jax.nn.sigmoid(_dot(u, wg_ref[:, i * d:(i + 1) * d]) + bg_ref[:, i * d:(i + 1) * d])
        merged = merged + gate * _dot(y, w_ref[...])
    m = _dot(merged.astype(BF16), wo_ref[...])
    o_ref[...] = x + _rms(m, post_g_ref[...])


def _mixer_out(h, y_a, o_b, lse_b, y_c, pre_g, w_gate, b_gate, w_a, w_b, w_c, w_o, post_g, *, tm=512):
    t, d = h.shape
    row = lambda i: (i, 0)
    tile = lambda a: pl.BlockSpec((tm, a.shape[1]), row)
    acts = [h, y_a, *o_b, *lse_b, y_c]
    consts = [pre_g, w_gate, b_gate, w_a, w_b, w_c, w_o, post_g]
    return pl.pallas_call(
        _mixer_out_body,
        grid=(t // tm,),
        in_specs=[tile(a) for a in acts] + [_resident(c.shape) for c in consts],
        out_specs=pl.BlockSpec((tm, d), row),
        out_shape=jax.ShapeDtypeStruct((t, d), F32),
        compiler_params=_params(1),
        name="mixer_out",
    )(*acts, *consts)


def _rope_tables(positions):
    half = ROT_DIM // 2
    inv_freq = ROPE_THETA ** (-jnp.arange(0, ROT_DIM, 2, dtype=F32) / ROT_DIM)
    ang = positions.astype(F32).reshape(-1, 1) * inv_freq
    cos, sin = jnp.cos(ang), jnp.sin(ang)
    t = cos.shape[0]
    pad = lambda n: jnp.zeros((t, n), F32)
    cos_t = jnp.concatenate([cos, cos, jnp.ones((t, HEAD_DIM - 2 * half), F32)], axis=1)
    sinp_t = jnp.concatenate([pad(half), sin, pad(HEAD_DIM - 2 * half)], axis=1)
    sinm_t = jnp.concatenate([-sin, pad(HEAD_DIM - half)], axis=1)
    rep = LANES // HEAD_DIM
    return tuple(jnp.tile(a, (1, rep)) for a in (cos_t, sinp_t, sinm_t))


def _split_w_in(w_in):
    d = w_in.shape[0]
    sizes = (A_HEADS * HEAD_DIM, A_KV_RANK, IDX_HEADS * IDX_DIM, IDX_DIM, IDX_HEADS,
             3 * B_HEADS * HEAD_DIM, C_HEADS * C_HEAD_K, C_HEADS * C_HEAD_K,
             C_HEADS * C_HEAD_V, C_HEADS * C_HEAD_V, C_ALPHA_RANK)
    names = ("a_q", "ckv", "i_q", "i_k", "i_w", "b_qkv", "c_q", "c_k", "c_v", "c_g", "c_a")
    parts, o = {}, 0
    for name, n in zip(names, sizes):
        parts[name] = w_in[:, o:o + n]
        o += n
    bw = B_HEADS * HEAD_DIM
    parts["b_q"], parts["b_k"], parts["b_v"] = (parts["b_qkv"][:, j * bw:(j + 1) * bw] for j in range(3))
    used = IDX_DIM + IDX_HEADS + C_ALPHA_RANK
    parts["small"] = jnp.concatenate(
        [parts["i_k"], parts["i_w"], parts["c_a"], jnp.zeros((d, LANES - used), w_in.dtype)], axis=1)
    return jnp.concatenate([parts[name] for name, _ in _IN_COLS], axis=1).astype(BF16)


def kernel(x, positions, ffn1_pre_g, ffn1_w_in, ffn1_w_out, ffn1_post_g, mix_pre_g, w_in, a_kv_norm_g, a_w_kv_up, c_w_alpha_up, c_b_alpha, c_norm_g, w_branch_a, w_branch_b, w_branch_c, w_gate, b_gate, w_out, mix_post_g, ffn2_pre_g, ffn2_w_in, ffn2_w_out, ffn2_post_g):
    bsz, seq, d = x.shape
    depth = w_in.shape[0]
    top_k = min(TOPK_MAX, seq // 4)
    tables = _rope_tables(positions)
    row = lambda a: a.reshape(1, -1)
    h = x.reshape(bsz * seq, d)
    for l in range(depth):
        h = _ffn(h, row(ffn1_pre_g[l]), ffn1_w_in[l].astype(BF16), ffn1_w_out[l].astype(BF16),
                 row(ffn1_post_g[l]))

        kv_up = a_w_kv_up[l].reshape(A_KV_RANK, A_HEADS, 2, HEAD_DIM)
        w_kup = kv_up[:, :, 0].reshape(A_KV_RANK, A_HEADS * HEAD_DIM).astype(BF16)
        w_vup = kv_up[:, :, 1].reshape(A_KV_RANK, A_HEADS * HEAD_DIM).astype(BF16)
        w_alpha = jnp.zeros((LANES, C_HEADS * C_HEAD_K), F32)
        w_alpha = w_alpha.at[SMALL_CA:SMALL_CA + C_ALPHA_RANK].set(c_w_alpha_up[l]).astype(BF16)
        p = _mixer_in(h, row(mix_pre_g[l]), _split_w_in(w_in[l]), *tables, row(a_kv_norm_g[l]),
                      w_kup, w_vup, w_alpha, row(c_b_alpha[l]))

        seq3 = lambda a: a.reshape(bsz, seq, a.shape[-1])
        y_a = _dsa(seq3(p["q_i"]), seq3(p["w_i"]), seq3(p["q_a"]), seq3(p["k_i"]), seq3(p["k_a"]),
                   seq3(p["v_a"]), top_k=top_k).reshape(bsz * seq, -1)
        o_b, lse_b = [], []
        for g, (_, dil) in enumerate(B_GROUPS):
            o_g, lse_g = _dilated_group(seq3(p["b_q"]), seq3(p["b_k"]), seq3(p["b_v"]), g, dil)
            o_b.append(o_g)
            lse_b.append(lse_g)
        y_c = _gla(seq3(p["c_q"]), seq3(p["c_k"]), seq3(p["c_v"]), seq3(p["c_g"]), seq3(p["la"]),
                   row(c_norm_g[l])).reshape(bsz * seq, -1)

        h = _mixer_out(h, y_a, o_b, lse_b, y_c, row(mix_pre_g[l]), w_gate[l].astype(BF16),
                       row(b_gate[l]), w_branch_a[l].astype(BF16), w_branch_b[l].astype(BF16),
                       w_branch_c[l].astype(BF16), w_out[l].astype(BF16), row(mix_post_g[l]))

        h = _ffn(h, row(ffn2_pre_g[l]), ffn2_w_in[l].astype(BF16), ffn2_w_out[l].astype(BF16),
                 row(ffn2_post_g[l]))
    return h.reshape(bsz, seq, d)
```

```python
import functools

import jax
import jax.numpy as jnp
from jax import lax
from jax.experimental import pallas as pl
from jax.experimental.pallas import tpu as pltpu

F32 = jnp.float32
BF16 = jnp.bfloat16

HEAD_DIM = 64
ROT_DIM = HEAD_DIM // 4
ROPE_THETA = 500000.0
RMS_EPS = 1e-6
MACARON_WEIGHT = 0.5

A_HEADS = 8
A_KV_RANK = 128
IDX_HEADS = 8
IDX_DIM = 64
TOPK_MAX = 256

B_GROUPS = ((128, 1), (512, 4), (2048, 16))
B_HEADS_PER_GROUP = 4
B_HEADS = B_HEADS_PER_GROUP * len(B_GROUPS)

C_HEADS = 4
C_HEAD_K = 64
C_HEAD_V = 128
C_ALPHA_RANK = 16
C_GATE_TAU = 16.0
C_CHUNK = 64

LANES = 128
Q_BLOCK = 128
DSA_KEY_CHUNK = 512
BISECT_MAX_ITERS = 48
BISECT_UNROLL = 4
COUNT_SLABS = 8
MASKED_LOGIT = -1e30
LOG2_E = 1.4426950408889634
VMEM_LIMIT = 60 * 1024 * 1024

SMALL_IK = 0
SMALL_IW = IDX_DIM
SMALL_CA = IDX_DIM + IDX_HEADS


def _dot(a, b):
    return jnp.dot(a, b, preferred_element_type=F32)


def _dot_nt(a, b):
    return lax.dot_general(a, b, (((1,), (1,)), ((), ())), preferred_element_type=F32)


def _dot_tn(a, b):
    return lax.dot_general(a, b, (((0,), (0,)), ((), ())), preferred_element_type=F32)


def _rms(x, g):
    return x * lax.rsqrt(jnp.mean(x * x, axis=-1, keepdims=True) + RMS_EPS) * g


def _silu(x):
    return x * jax.nn.sigmoid(x)


def _params(n_grid):
    return pltpu.CompilerParams(dimension_semantics=("arbitrary",) * n_grid,
                                vmem_limit_bytes=VMEM_LIMIT)


def _resident(shape):
    zeros = (0,) * len(shape)
    return pl.BlockSpec(shape, lambda *_: zeros, pipeline_mode=pl.Buffered(1))


def _half_mask(shape, upper):
    lane = lax.broadcasted_iota(jnp.int32, shape, len(shape) - 1) % LANES
    return (lane >= HEAD_DIM) if upper else (lane < HEAD_DIM)


def _ffn_body(h_ref, pre_g_ref, w_in_ref, w_out_ref, post_g_ref, o_ref, *, d_ff, tf):
    x = h_ref[...]
    xn = _rms(x, pre_g_ref[...]).astype(BF16)
    acc = jnp.zeros(x.shape, F32)
    for j in range(d_ff // tf):
        gate = _dot(xn, w_in_ref[:, j * tf:(j + 1) * tf])
        up = _dot(xn, w_in_ref[:, d_ff + j * tf:d_ff + (j + 1) * tf])
        act = (_silu(gate) * up).astype(BF16)
        acc = acc + _dot(act, w_out_ref[j * tf:(j + 1) * tf, :])
    o_ref[...] = x + MACARON_WEIGHT * _rms(acc, post_g_ref[...])


def _ffn(h, pre_g, w_in, w_out, post_g, *, tm=512):
    t, d = h.shape
    d_ff = w_out.shape[0]
    tf = d_ff // 2 if (d_ff // 2) % LANES == 0 else d_ff
    row = lambda i: (i, 0)
    return pl.pallas_call(
        functools.partial(_ffn_body, d_ff=d_ff, tf=tf),
        grid=(t // tm,),
        in_specs=[pl.BlockSpec((tm, d), row), _resident((1, d)), _resident(w_in.shape),
                  _resident(w_out.shape), _resident((1, d))],
        out_specs=pl.BlockSpec((tm, d), row),
        out_shape=jax.ShapeDtypeStruct((t, d), F32),
        compiler_params=_params(1),
        name="ffn",
    )(h, pre_g, w_in, w_out, post_g)


_IN_COLS = (("a_q", 512), ("i_q", 512), ("b_q", 768), ("b_k", 768), ("b_v", 768),
            ("c_q", 256), ("c_k", 256), ("c_v", 512), ("c_g", 512), ("ckv", 128), ("small", 128))


def _in_col_offsets():
    offs, o = {}, 0
    for name, n in _IN_COLS:
        offs[name] = (o, o + n)
        o += n
    return offs, o


def _rope(x, cos_t, sinp_t, sinm_t):
    outs = []
    for i in range(x.shape[1] // LANES):
        xs = x[:, i * LANES:(i + 1) * LANES]
        outs.append(xs * cos_t + pltpu.roll(xs, 8, 1) * sinp_t + pltpu.roll(xs, LANES - 8, 1) * sinm_t)
    return outs[0] if len(outs) == 1 else jnp.concatenate(outs, axis=1)


def _log_sigmoid(x):
    return jnp.minimum(x, 0.0) - jnp.log1p(jnp.exp(-jnp.abs(x)))


def _mixer_in_body(h_ref, g_ref, w_ref, cos_ref, sinp_ref, sinm_ref, kvg_ref, wk_ref, wv_ref,
                   wal_ref, bal_ref,
                   qa_ref, ka_ref, vt_ref, qi_ref, ki_ref, small_ref, bq_ref, bk_ref, bv_ref,
                   cq_ref, ck_ref, cv_ref, cg_ref, la_ref):
    offs, _ = _in_col_offsets()
    u = _rms(h_ref[...], g_ref[...]).astype(BF16)
    cos_t, sinp_t, sinm_t = cos_ref[...], sinp_ref[...], sinm_ref[...]

    def proj(name):
        lo, hi = offs[name]
        return _dot(u, w_ref[:, lo:hi])

    rope = lambda x: _rope(x, cos_t, sinp_t, sinm_t)
    q_scale = HEAD_DIM ** -0.5
    qa_ref[...] = (rope(proj("a_q")) * (q_scale * LOG2_E)).astype(BF16)
    qi_ref[...] = (rope(proj("i_q")) * (IDX_DIM ** -0.5)).astype(BF16)
    bq_ref[...] = (rope(proj("b_q")) * q_scale).astype(BF16)
    bk_ref[...] = rope(proj("b_k")).astype(BF16)
    bv_ref[...] = proj("b_v").astype(BF16)
    cq_ref[...] = proj("c_q") * (C_HEAD_K ** -0.5)
    ck_ref[...] = proj("c_k")
    cv_ref[...] = proj("c_v")
    cg_ref[...] = proj("c_g")

    ckv = _rms(proj("ckv"), kvg_ref[...]).astype(BF16)
    ka_ref[...] = rope(_dot(ckv, wk_ref[...])).astype(BF16)
    vt_ref[0] = _dot_nt(wv_ref[...], ckv).astype(BF16)

    small = proj("small")
    small_r = rope(small)
    ki_ref[...] = jnp.where(_half_mask(small.shape, False), small_r,
                            pltpu.roll(small_r, HEAD_DIM, 1)).astype(BF16)
    small_ref[...] = small
    la = _dot(small.astype(BF16), wal_ref[...]) + bal_ref[...]
    la_ref[...] = _log_sigmoid(la) * (1.0 / C_GATE_TAU)


def _mixer_in(h, g, w_big, cos_t, sinp_t, sinm_t, kv_g, w_kup, w_vup_t, w_alpha, b_alpha):
    t, d = h.shape
    tm = DSA_KEY_CHUNK
    row = lambda i: (i, 0)
    out_defs = [("q_a", 512, BF16), ("k_a", 512, BF16), ("v_t", None, BF16), ("q_i", 512, BF16),
                ("k_i", 128, BF16), ("small", LANES, F32), ("b_q", 768, BF16), ("b_k", 768, BF16),
                ("b_v", 768, BF16), ("c_q", 256, F32), ("c_k", 256, F32), ("c_v", 512, F32),
                ("c_g", 512, F32), ("la", 256, F32)]
    out_specs = [pl.BlockSpec((1, 512, tm), lambda i: (i, 0, 0)) if n is None
                 else pl.BlockSpec((tm, n), row) for _, n, _ in out_defs]
    out_shape = [jax.ShapeDtypeStruct((t // tm, 512, tm) if n is None else (t, n), dt)
                 for _, n, dt in out_defs]
    w_vup = w_vup_t
    outs = pl.pallas_call(
        _mixer_in_body,
        grid=(t // tm,),
        in_specs=[pl.BlockSpec((tm, d), row), _resident((1, d)), _resident(w_big.shape),
                  pl.BlockSpec((tm, LANES), row), pl.BlockSpec((tm, LANES), row),
                  pl.BlockSpec((tm, LANES), row), _resident(kv_g.shape), _resident(w_kup.shape),
                  _resident(w_vup.shape), _resident(w_alpha.shape), _resident(b_alpha.shape)],
        out_specs=out_specs,
        out_shape=out_shape,
        compiler_params=_params(1),
        name="mixer_in",
    )(h, g, w_big, cos_t, sinp_t, sinm_t, kv_g, w_kup, w_vup, w_alpha, b_alpha)
    return {name: o for (name, _, _), o in zip(out_defs, outs)}


def _masked_pair_rows(x, n_heads):
    out = []
    for h in range(n_heads):
        pair = x[:, (h // 2) * LANES:(h // 2 + 1) * LANES]
        out.append(jnp.where(_half_mask(pair.shape, h % 2 == 1), pair, jnp.zeros_like(pair)))
    return out


def _dsa_body(qi_ref, small_ref, qa_ref, ki_ref, ka_ref, vt_ref, o_ref,
              score_ref, acc_ref, qm_ref, s0_ref, s1_ref, *, seq, top_k):
    kc = DSA_KEY_CHUNK
    qb = Q_BLOCK
    i = pl.program_id(1)
    n_chunks = (i * qb + qb + kc - 1) // kc
    q_pos = i * qb + lax.broadcasted_iota(jnp.int32, (1, qb), 1)
    key_off = lax.broadcasted_iota(jnp.int32, (kc, qb), 0)

    n_pairs = A_HEADS // 2
    qi_heads = _masked_pair_rows(qi_ref[0], IDX_HEADS)
    for pr in range(n_pairs):
        qm_ref[pr] = jnp.concatenate([qi_heads[2 * pr], qi_heads[2 * pr + 1]], axis=0)
    w_t = jnp.transpose(small_ref[0])[SMALL_IW:SMALL_IW + IDX_HEADS, :] * (IDX_HEADS ** -0.5)

    def logits_stage(k_ref, c, s_ref):
        off = pl.multiple_of(jnp.minimum(c, n_chunks - 1) * kc, kc)
        n_k = k_ref.shape[2] // LANES
        for pr in range(n_pairs):
            k_c = k_ref[0, pl.ds(off, kc), (pr % n_k) * LANES:(pr % n_k + 1) * LANES]
            s_ref[pr] = _dot_nt(k_c, qm_ref[pr])

    def skewed(first, stage, carry):
        first(0, s0_ref)

        def step(j, carry):
            first(2 * j + 1, s1_ref)
            carry = stage(2 * j, s0_ref, carry)
            first(2 * j + 2, s0_ref)
            return stage(2 * j + 1, s1_ref, carry)
        return lax.fori_loop(0, (n_chunks + 1) // 2, step, carry)

    def score_stage(c, s_ref, carry):
        mx, mn = carry
        cc = jnp.minimum(c, n_chunks - 1)
        sc = jnp.zeros((kc, qb), F32)
        for h in range(IDX_HEADS):
            logit = s_ref[h // 2, :, (h % 2) * qb:(h % 2 + 1) * qb]
            sc = sc + jnp.maximum(logit, 0.0) * w_t[h:h + 1, :]
        causal = (cc * kc + key_off) <= q_pos
        score_ref[cc] = jnp.where(causal, sc, -jnp.inf)
        mx = jnp.maximum(mx, jnp.max(jnp.where(causal, sc, -jnp.inf), axis=0, keepdims=True))
        mn = jnp.minimum(mn, jnp.min(jnp.where(causal, sc, jnp.inf), axis=0, keepdims=True))
        return mx, mn

    row_max, row_min = skewed(
        functools.partial(logits_stage, ki_ref), score_stage,
        (jnp.full((1, qb), -jnp.inf, F32), jnp.full((1, qb), jnp.inf, F32)))

    def count(pred):
        def step(c, acc):
            hit = jnp.where(pred(score_ref[c], c), 1.0, 0.0)
            return acc + jnp.sum(hit.reshape(COUNT_SLABS, kc // COUNT_SLABS, qb), axis=0)
        acc = lax.fori_loop(0, n_chunks, step, jnp.zeros((kc // COUNT_SLABS, qb), F32))
        return jnp.sum(acc, axis=0, keepdims=True)

    k_f = float(top_k)
    n_causal = (q_pos + 1).astype(F32)
    few = n_causal <= k_f
    lo0 = row_min
    hi0 = row_max + jnp.abs(row_max) + 1.0

    def bisect_cond(st):
        return jnp.logical_and(st[0] < BISECT_MAX_ITERS, st[5] > 0.5)

    def bisect_step(st):
        it, lo, hi, c_lo, c_hi, _ = st
        stalled = jnp.zeros((1, qb), jnp.bool_)
        for _ in range(BISECT_UNROLL):
            mid = lo + (hi - lo) * 0.5
            cnt = count(lambda x, c: x >= mid)
            ge = cnt >= k_f
            stalled = stalled | (mid <= lo) | (mid >= hi)
            lo, hi = jnp.where(ge, mid, lo), jnp.where(ge, hi, mid)
            c_lo, c_hi = jnp.where(ge, cnt, c_lo), jnp.where(ge, c_hi, cnt)
        done = few | (c_lo == k_f) | stalled
        return it + BISECT_UNROLL, lo, hi, c_lo, c_hi, jnp.max(jnp.where(done, 0.0, 1.0))

    _, lo, hi, c_lo, c_hi, _ = lax.while_loop(
        bisect_cond, bisect_step,
        (jnp.int32(0), lo0, hi0, n_causal, jnp.zeros((1, qb), F32),
         jnp.max(jnp.where(few, 0.0, 1.0))))

    tied = jnp.logical_and(jnp.logical_not(few), c_lo > k_f)
    need = k_f - c_hi

    def tie_search(_):
        def step(_, st):
            j_lo, j_hi = st
            mid = jnp.floor((j_lo + j_hi) * 0.5)
            cnt = count(lambda x, c: (x >= lo) & (x < hi) & ((c * kc + key_off).astype(F32) <= mid))
            ok = cnt >= need
            return jnp.where(ok, j_lo, mid), jnp.where(ok, mid, j_hi)
        n_steps = max(1, (seq - 1).bit_length()) + 1
        _, j_hi = lax.fori_loop(0, n_steps, step,
                                (jnp.full((1, qb), -1.0, F32), jnp.full((1, qb), float(seq - 1), F32)))
        return j_hi

    any_tied = jnp.max(jnp.where(tied, 1.0, 0.0)) > 0.5
    j_cut = lax.cond(any_tied, tie_search, lambda _: jnp.full((1, qb), float(seq), F32), 0)
    j_cut = jnp.where(tied, j_cut, float(seq))
    thr_hi = jnp.where(tied, hi, lo)

    qa_heads = _masked_pair_rows(qa_ref[0], A_HEADS)
    for pr in range(n_pairs):
        qm_ref[pr] = jnp.concatenate([qa_heads[2 * pr], qa_heads[2 * pr + 1]], axis=0)
    acc_ref[...] = jnp.zeros(acc_ref.shape, F32)

    def softmax_pv_stage(c, s_ref, carry):
        m_all, l_all = carry
        cc = jnp.minimum(c, n_chunks - 1)
        x = score_ref[cc]
        sel = (x >= thr_hi) | ((x >= lo) & ((cc * kc + key_off).astype(F32) <= j_cut))
        bias = jnp.where(sel & (c < n_chunks), 0.0, MASKED_LOGIT)
        m_out, l_out = [], []
        for pr in range(n_pairs):
            p_halves, alphas = [], []
            for e in range(2):
                h = 2 * pr + e
                s = s_ref[pr, :, e * qb:(e + 1) * qb] + bias
                m_new = jnp.maximum(m_all[h], jnp.max(s, axis=0, keepdims=True))
                alpha = jnp.exp2(m_all[h] - m_new)
                p = jnp.exp2(s - m_new)
                l_out.append(alpha * l_all[h] + jnp.sum(p, axis=0, keepdims=True))
                m_out.append(m_new)
                p_halves.append(p.astype(BF16))
                alphas.append(alpha)
            v_c = vt_ref[0, cc, pr * LANES:(pr + 1) * LANES, :]
            pv = _dot(v_c, jnp.concatenate(p_halves, axis=1))
            acc_ref[pr] = acc_ref[pr] * jnp.concatenate(alphas, axis=1) + pv
        return tuple(m_out), tuple(l_out)

    m0 = tuple(jnp.full((1, qb), MASKED_LOGIT, F32) for _ in range(A_HEADS))
    l0 = tuple(jnp.zeros((1, qb), F32) for _ in range(A_HEADS))
    _, l_fin = skewed(functools.partial(logits_stage, ka_ref), softmax_pv_stage, (m0, l0))

    for pr in range(n_pairs):
        a = acc_ref[pr]
        o_t = jnp.concatenate([a[:HEAD_DIM, :qb] / l_fin[2 * pr],
                               a[HEAD_DIM:, qb:] / l_fin[2 * pr + 1]], axis=0)
        o_ref[0, :, pr * LANES:(pr + 1) * LANES] = jnp.transpose(o_t).astype(BF16)


def _dsa(q_i, small, q_a, k_i, k_a, v_t, *, top_k):
    bsz, seq, _ = q_a.shape
    qb, kc = Q_BLOCK, DSA_KEY_CHUNK
    blk = lambda b, i: (b, i, 0)
    whole = lambda shape: pl.BlockSpec((1,) + shape, lambda b, i: (b,) + (0,) * len(shape),
                                       pipeline_mode=pl.Buffered(1))
    return pl.pallas_call(
        functools.partial(_dsa_body, seq=seq, top_k=top_k),
        grid=(bsz, seq // qb),
        in_specs=[pl.BlockSpec((1, qb, 512), blk), pl.BlockSpec((1, qb, LANES), blk),
                  pl.BlockSpec((1, qb, 512), blk), whole((seq, LANES)), whole((seq, 512)),
                  whole((seq // kc, 512, kc))],
        out_specs=pl.BlockSpec((1, qb, 512), blk),
        out_shape=jax.ShapeDtypeStruct((bsz, seq, 512), BF16),
        scratch_shapes=[pltpu.VMEM((seq // kc, kc, qb), F32),
                        pltpu.VMEM((A_HEADS // 2, LANES, 2 * qb), F32),
                        pltpu.VMEM((A_HEADS // 2, 2 * qb, LANES), BF16),
                        pltpu.VMEM((A_HEADS // 2, kc, 2 * qb), F32),
                        pltpu.VMEM((A_HEADS // 2, kc, 2 * qb), F32)],
        compiler_params=_params(2),
        name="dsa",
    )(q_i, small, q_a, k_i, k_a, v_t)


def _dilated_body(q_ref, kp_ref, k_ref, vp_ref, v_ref, o_ref, lse_ref, *, tq):
    qb = Q_BLOCK
    n = pl.program_id(2)
    row = lax.broadcasted_iota(jnp.int32, (qb, 2 * qb), 0)
    col = lax.broadcasted_iota(jnp.int32, (qb, 2 * qb), 1)
    rel = row + qb - col
    band = (rel >= 0) & (rel <= qb)
    n_pairs = B_HEADS_PER_GROUP // 2
    for s in range(tq // qb):
        q = q_ref[0, s * qb:(s + 1) * qb, :]
        if s == 0:
            k_prev, v_prev = kp_ref[0], vp_ref[0]
            valid = band & ((col >= qb) | (n > 0))
        else:
            k_prev = k_ref[0, (s - 1) * qb:s * qb, :]
            v_prev = v_ref[0, (s - 1) * qb:s * qb, :]
            valid = band
        kk = jnp.concatenate([k_prev, k_ref[0, s * qb:(s + 1) * qb, :]], axis=0)
        vv = jnp.concatenate([v_prev, v_ref[0, s * qb:(s + 1) * qb, :]], axis=0)
        o_parts, lse_parts = [], []
        for pr in range(n_pairs):
            q_pair = q[:, pr * LANES:(pr + 1) * LANES]
            k_pair = kk[:, pr * LANES:(pr + 1) * LANES]
            v_pair = vv[:, pr * LANES:(pr + 1) * LANES]
            o_h, lse_h = [], []
            for odd in (False, True):
                q_h = jnp.where(_half_mask(q_pair.shape, odd), q_pair, jnp.zeros_like(q_pair))
                sc = jnp.where(valid, _dot_nt(q_h, k_pair), -jnp.inf)
                m = jnp.max(sc, axis=1, keepdims=True)
                p = jnp.exp(sc - m)
                l = jnp.sum(p, axis=1, keepdims=True)
                o_h.append(_dot(p.astype(BF16), v_pair) / l)
                lse_h.append(jnp.broadcast_to(m + jnp.log(l), (qb, LANES)))
            lower = _half_mask((qb, LANES), False)
            o_parts.append(jnp.where(lower, o_h[0], o_h[1]))
            lse_parts.append(jnp.where(lower, lse_h[0], lse_h[1]))
        o_ref[0, s * qb:(s + 1) * qb, :] = jnp.concatenate(o_parts, axis=1)
        lse_ref[0, s * qb:(s + 1) * qb, :] = jnp.concatenate(lse_parts, axis=1)


def _dilated_group(b_q, b_k, b_v, g, dil):
    bsz, seq, width = b_q.shape
    sub = seq // dil
    qb = Q_BLOCK
    tq = min(512, sub)
    gw = B_HEADS_PER_GROUP * HEAD_DIM
    n_grp = width // gw
    view = lambda a: a.reshape(bsz, sub, dil * width)
    cur = lambda b, r, n: (b, n, r * n_grp + g)
    prev = lambda b, r, n: (b, jnp.maximum(n * (tq // qb) - 1, 0), r * n_grp + g)
    out = lambda b, r, n: (b, n, r)
    o, lse = pl.pallas_call(
        functools.partial(_dilated_body, tq=tq),
        grid=(bsz, dil, sub // tq),
        in_specs=[pl.BlockSpec((1, tq, gw), cur),
                  pl.BlockSpec((1, qb, gw), prev), pl.BlockSpec((1, tq, gw), cur),
                  pl.BlockSpec((1, qb, gw), prev), pl.BlockSpec((1, tq, gw), cur)],
        out_specs=[pl.BlockSpec((1, tq, gw), out), pl.BlockSpec((1, tq, gw), out)],
        out_shape=[jax.ShapeDtypeStruct((bsz, sub, dil * gw), F32)] * 2,
        compiler_params=_params(3),
        name=f"dilated_d{dil}",
    )(view(b_q), view(b_k), view(b_k), view(b_v), view(b_v))
    return o.reshape(bsz * seq, gw), lse.reshape(bsz * seq, gw)


def _gla_body(q_ref, k_ref, v_ref, g_ref, la_ref, ng_ref, o_ref, st_ref, *, n_sub):
    cs = C_CHUNK

    @pl.when(pl.program_id(1) == 0)
    def _():
        st_ref[...] = jnp.zeros(st_ref.shape, F32)

    r_i = lax.broadcasted_iota(jnp.int32, (cs, cs), 0)
    c_i = lax.broadcasted_iota(jnp.int32, (cs, cs), 1)
    tril = r_i >= c_i
    tri_ones = jnp.where(tril, 1.0, 0.0).astype(BF16)
    ng = ng_ref[...]
    for c in range(n_sub):
        rows = slice(c * cs, (c + 1) * cs)
        la = la_ref[0, rows, :]
        la_hi = la.astype(BF16)
        la_lo = (la - la_hi.astype(F32)).astype(BF16)
        b = _dot(tri_ones, la_hi) + _dot(tri_ones, la_lo)
        b_last = b[cs - 1:cs, :]
        q = q_ref[0, rows, :]
        k = k_ref[0, rows, :]
        q_dec = q * jnp.exp(b)
        k_dec = (k * jnp.exp(-b)).astype(BF16)
        k_end = (k * jnp.exp(b_last - b)).astype(BF16)
        decay = jnp.exp(b_last)
        o_heads = []
        for h in range(C_HEADS):
            p_lo = (h // 2) * LANES
            q_pair = q_dec[:, p_lo:p_lo + LANES]
            q_h = jnp.where(_half_mask(q_pair.shape, h % 2 == 1), q_pair, 0.0).astype(BF16)
            v_h = v_ref[0, rows, h * C_HEAD_V:(h + 1) * C_HEAD_V].astype(BF16)
            attn = jnp.where(tril, _dot_nt(q_h, k_dec[:, p_lo:p_lo + LANES]), 0.0)
            st = st_ref[h]
            o_h = _dot(attn.astype(BF16), v_h) + _dot_nt(q_h, st.astype(BF16))
            st_ref[h] = st * decay[:, p_lo:p_lo + LANES] + _dot_tn(v_h, k_end[:, p_lo:p_lo + LANES])
            gate = g_ref[0, rows, h * C_HEAD_V:(h + 1) * C_HEAD_V]
            o_heads.append(_rms(o_h, ng) * _silu(gate))
        o_ref[0, rows, :] = jnp.concatenate(o_heads, axis=1).astype(BF16)


def _gla(c_q, c_k, c_v, c_g, la, norm_g, *, n_sub=8):
    bsz, seq, _ = c_q.shape
    rows = n_sub * C_CHUNK
    blk = lambda b, n: (b, n, 0)
    kw, vw = C_HEADS * C_HEAD_K, C_HEADS * C_HEAD_V
    return pl.pallas_call(
        functools.partial(_gla_body, n_sub=n_sub),
        grid=(bsz, seq // rows),
        in_specs=[pl.BlockSpec((1, rows, kw), blk), pl.BlockSpec((1, rows, kw), blk),
                  pl.BlockSpec((1, rows, vw), blk), pl.BlockSpec((1, rows, vw), blk),
                  pl.BlockSpec((1, rows, kw), blk), _resident(norm_g.shape)],
        out_specs=pl.BlockSpec((1, rows, vw), blk),
        out_shape=jax.ShapeDtypeStruct((bsz, seq, vw), BF16),
        scratch_shapes=[pltpu.VMEM((C_HEADS, C_HEAD_V, LANES), F32)],
        compiler_params=_params(2),
        name="gla",
    )(c_q, c_k, c_v, c_g, la, norm_g)


def _mixer_out_body(h_ref, ya_ref, ob0_ref, ob1_ref, ob2_ref, ls0_ref, ls1_ref, ls2_ref, yc_ref,
                    pre_g_ref, wg_ref, bg_ref, wa_ref, wb_ref, wc_ref, wo_ref, post_g_ref, o_ref):
    x = h_ref[...]
    d = x.shape[1]
    u = _rms(x, pre_g_ref[...]).astype(BF16)

    ls = [ls0_ref[...], ls1_ref[...], ls2_ref[...]]
    ob = [ob0_ref[...], ob1_ref[...], ob2_ref[...]]
    mx = jnp.maximum(jnp.maximum(ls[0], ls[1]), ls[2])
    ew = [jnp.exp(v - mx) for v in ls]
    y_b = ((ew[0] * ob[0] + ew[1] * ob[1] + ew[2] * ob[2]) / (ew[0] + ew[1] + ew[2])).astype(BF16)

    branches = ((ya_ref[...], wa_ref), (y_b, wb_ref), (yc_ref[...], wc_ref))
    merged = jnp.zeros(x.shape, F32)
    for i, (y, w_ref) in enumerate(branches):
        gate = jax.nn.sigmoid(_dot(u, wg_ref[:, i * d:(i + 1) * d]) + bg_ref[:, i * d:(i + 1) * d])
        merged = merged + gate * _dot(y, w_ref[...])
    m = _dot(merged.astype(BF16), wo_ref[...])
    o_ref[...] = x + _rms(m, post_g_ref[...])


def _mixer_out(h, y_a, o_b, lse_b, y_c, pre_g, w_gate, b_gate, w_a, w_b, w_c, w_o, post_g, *, tm=512):
    t, d = h.shape
    row = lambda i: (i, 0)
    tile = lambda a: pl.BlockSpec((tm, a.shape[1]), row)
    acts = [h, y_a, *o_b, *lse_b, y_c]
    consts = [pre_g, w_gate, b_gate, w_a, w_b, w_c, w_o, post_g]
    return pl.pallas_call(
        _mixer_out_body,
        grid=(t // tm,),
        in_specs=[tile(a) for a in acts] + [_resident(c.shape) for c in consts],
        out_specs=pl.BlockSpec((tm, d), row),
        out_shape=jax.ShapeDtypeStruct((t, d), F32),
        compiler_params=_params(1),
        name="mixer_out",
    )(*acts, *consts)


def _rope_tables(positions):
    half = ROT_DIM // 2
    inv_freq = ROPE_THETA ** (-jnp.arange(0, ROT_DIM, 2, dtype=F32) / ROT_DIM)
    ang = positions.astype(F32).reshape(-1, 1) * inv_freq
    cos, sin = jnp.cos(ang), jnp.sin(ang)
    t = cos.shape[0]
    pad = lambda n: jnp.zeros((t, n), F32)
    cos_t = jnp.concatenate([cos, cos, jnp.ones((t, HEAD_DIM - 2 * half), F32)], axis=1)
    sinp_t = jnp.concatenate([pad(half), sin, pad(HEAD_DIM - 2 * half)], axis=1)
    sinm_t = jnp.concatenate([-sin, pad(HEAD_DIM - half)], axis=1)
    rep = LANES // HEAD_DIM
    return tuple(jnp.tile(a, (1, rep)) for a in (cos_t, sinp_t, sinm_t))


def _split_w_in(w_in):
    d = w_in.shape[0]
    sizes = (A_HEADS * HEAD_DIM, A_KV_RANK, IDX_HEADS * IDX_DIM, IDX_DIM, IDX_HEADS,
             3 * B_HEADS * HEAD_DIM, C_HEADS * C_HEAD_K, C_HEADS * C_HEAD_K,
             C_HEADS * C_HEAD_V, C_HEADS * C_HEAD_V, C_ALPHA_RANK)
    names = ("a_q", "ckv", "i_q", "i_k", "i_w", "b_qkv", "c_q", "c_k", "c_v", "c_g", "c_a")
    parts, o = {}, 0
    for name, n in zip(names, sizes):
        parts[name] = w_in[:, o:o + n]
        o += n
    bw = B_HEADS * HEAD_DIM
    parts["b_q"], parts["b_k"], parts["b_v"] = (parts["b_qkv"][:, j * bw:(j + 1) * bw] for j in range(3))
    used = IDX_DIM + IDX_HEADS + C_ALPHA_RANK
    parts["small"] = jnp.concatenate(
        [parts["i_k"], parts["i_w"], parts["c_a"], jnp.zeros((d, LANES - used), w_in.dtype)], axis=1)
    return jnp.concatenate([parts[name] for name, _ in _IN_COLS], axis=1).astype(BF16)


def kernel(x, positions, ffn1_pre_g, ffn1_w_in, ffn1_w_out, ffn1_post_g, mix_pre_g, w_in, a_kv_norm_g, a_w_kv_up, c_w_alpha_up, c_b_alpha, c_norm_g, w_branch_a, w_branch_b, w_branch_c, w_gate, b_gate, w_out, mix_post_g, ffn2_pre_g, ffn2_w_in, ffn2_w_out, ffn2_post_g):
    bsz, seq, d = x.shape
    depth = w_in.shape[0]
    top_k = min(TOPK_MAX, seq // 4)
    tables = _rope_tables(positions)
    row = lambda a: a.reshape(1, -1)
    h = x.reshape(bsz * seq, d)
    for l in range(depth):
        h = _ffn(h, row(ffn1_pre_g[l]), ffn1_w_in[l].astype(BF16), ffn1_w_out[l].astype(BF16),
                 row(ffn1_post_g[l]))

        kv_up = a_w_kv_up[l].reshape(A_KV_RANK, A_HEADS, 2, HEAD_DIM)
        w_kup = kv_up[:, :, 0].reshape(A_KV_RANK, A_HEADS * HEAD_DIM).astype(BF16)
        w_vup_t = kv_up[:, :, 1].reshape(A_KV_RANK, A_HEADS * HEAD_DIM).T.astype(BF16)
        w_alpha = jnp.zeros((LANES, C_HEADS * C_HEAD_K), F32)
        w_alpha = w_alpha.at[SMALL_CA:SMALL_CA + C_ALPHA_RANK].set(c_w_alpha_up[l]).astype(BF16)
        p = _mixer_in(h, row(mix_pre_g[l]), _split_w_in(w_in[l]), *tables, row(a_kv_norm_g[l]),
                      w_kup, w_vup_t, w_alpha, row(c_b_alpha[l]))

        seq3 = lambda a: a.reshape(bsz, seq, a.shape[-1])
        v_t = p["v_t"].reshape(bsz, seq // DSA_KEY_CHUNK, A_HEADS * HEAD_DIM, DSA_KEY_CHUNK)
        y_a = _dsa(seq3(p["q_i"]), seq3(p["small"]), seq3(p["q_a"]), seq3(p["k_i"]), seq3(p["k_a"]),
                   v_t, top_k=top_k).reshape(bsz * seq, -1)
        o_b, lse_b = [], []
        for g, (_, dil) in enumerate(B_GROUPS):
            o_g, lse_g = _dilated_group(seq3(p["b_q"]), seq3(p["b_k"]), seq3(p["b_v"]), g, dil)
            o_b.append(o_g)
            lse_b.append(lse_g)
        y_c = _gla(seq3(p["c_q"]), seq3(p["c_k"]), seq3(p["c_v"]), seq3(p["c_g"]), seq3(p["la"]),
                   row(c_norm_g[l])).reshape(bsz * seq, -1)

        h = _mixer_out(h, y_a, o_b, lse_b, y_c, row(mix_pre_g[l]), w_gate[l].astype(BF16),
                       row(b_gate[l]), w_branch_a[l].astype(BF16), w_branch_b[l].astype(BF16),
                       w_branch_c[l].astype(BF16), w_out[l].astype(BF16), row(mix_post_g[l]))

        h = _ffn(h, row(ffn2_pre_g[l]), ffn2_w_in[l].astype(BF16), ffn2_w_out[l].astype(BF16),
                 row(ffn2_post_g[l]))
    return h.reshape(bsz, seq, d)
```

```python
import functools

import jax
import jax.numpy as jnp
from jax import lax
from jax.experimental import pallas as pl
from jax.experimental.pallas import tpu as pltpu

F32 = jnp.float32
BF16 = jnp.bfloat16

HEAD_DIM = 64
ROT_DIM = HEAD_DIM // 4
ROPE_THETA = 500000.0
RMS_EPS = 1e-6
MACARON_WEIGHT = 0.5

A_HEADS = 8
A_KV_RANK = 128
IDX_HEADS = 8
IDX_DIM = 64
TOPK_MAX = 256

B_GROUPS = ((128, 1), (512, 4), (2048, 16))
B_HEADS_PER_GROUP = 4
B_HEADS = B_HEADS_PER_GROUP * len(B_GROUPS)

C_HEADS = 4
C_HEAD_K = 64
C_HEAD_V = 128
C_ALPHA_RANK = 16
C_GATE_TAU = 16.0
C_CHUNK = 64

LANES = 128
Q_BLOCK = 128
DSA_KEY_CHUNK = 512
BF16_SUBLANES = 16
VT_ROWS = LANES + BF16_SUBLANES
BISECT_MAX_ITERS = 48
BISECT_UNROLL = 3
COUNT_SLABS = 8
MASKED_LOGIT = -1e30
LOG2_E = 1.4426950408889634
VMEM_LIMIT = 60 * 1024 * 1024

SMALL_IK = 0
SMALL_IW = IDX_DIM
SMALL_CA = IDX_DIM + IDX_HEADS


def _dot(a, b):
    return jnp.dot(a, b, preferred_element_type=F32)


def _dot_nt(a, b):
    return lax.dot_general(a, b, (((1,), (1,)), ((), ())), preferred_element_type=F32)


def _dot_tn(a, b):
    return lax.dot_general(a, b, (((0,), (0,)), ((), ())), preferred_element_type=F32)


def _rms(x, g):
    return x * lax.rsqrt(jnp.mean(x * x, axis=-1, keepdims=True) + RMS_EPS) * g


def _silu(x):
    return x * jax.nn.sigmoid(x)


def _params(n_grid):
    return pltpu.CompilerParams(dimension_semantics=("arbitrary",) * n_grid,
                                vmem_limit_bytes=VMEM_LIMIT)


def _resident(shape):
    zeros = (0,) * len(shape)
    return pl.BlockSpec(shape, lambda *_: zeros, pipeline_mode=pl.Buffered(1))


def _half_mask(shape, upper):
    lane = lax.broadcasted_iota(jnp.int32, shape, len(shape) - 1) % LANES
    return (lane >= HEAD_DIM) if upper else (lane < HEAD_DIM)


def _ffn_body(h_ref, pre_g_ref, w_in_ref, w_out_ref, post_g_ref, o_ref, *, d_ff, tf):
    x = h_ref[...]
    xn = _rms(x, pre_g_ref[...]).astype(BF16)
    acc = jnp.zeros(x.shape, F32)
    for j in range(d_ff // tf):
        gate = _dot(xn, w_in_ref[:, j * tf:(j + 1) * tf])
        up = _dot(xn, w_in_ref[:, d_ff + j * tf:d_ff + (j + 1) * tf])
        act = (_silu(gate) * up).astype(BF16)
        acc = acc + _dot(act, w_out_ref[j * tf:(j + 1) * tf, :])
    o_ref[...] = x + MACARON_WEIGHT * _rms(acc, post_g_ref[...])


def _ffn(h, pre_g, w_in, w_out, post_g, *, tm=512):
    t, d = h.shape
    d_ff = w_out.shape[0]
    tf = d_ff // 2 if (d_ff // 2) % LANES == 0 else d_ff
    row = lambda i: (i, 0)
    return pl.pallas_call(
        functools.partial(_ffn_body, d_ff=d_ff, tf=tf),
        grid=(t // tm,),
        in_specs=[pl.BlockSpec((tm, d), row), _resident((1, d)), _resident(w_in.shape),
                  _resident(w_out.shape), _resident((1, d))],
        out_specs=pl.BlockSpec((tm, d), row),
        out_shape=jax.ShapeDtypeStruct((t, d), F32),
        compiler_params=_params(1),
        name="ffn",
    )(h, pre_g, w_in, w_out, post_g)


_IN_COLS = (("a_q", 512), ("i_q", 512), ("b_q", 768), ("b_k", 768), ("b_v", 768),
            ("c_q", 256), ("c_k", 256), ("c_v", 512), ("c_g", 512), ("ckv", 128), ("small", 128))


def _in_col_offsets():
    offs, o = {}, 0
    for name, n in _IN_COLS:
        offs[name] = (o, o + n)
        o += n
    return offs, o


def _rope(x, cos_t, sinp_t, sinm_t):
    outs = []
    for i in range(x.shape[1] // LANES):
        xs = x[:, i * LANES:(i + 1) * LANES]
        outs.append(xs * cos_t + pltpu.roll(xs, 8, 1) * sinp_t + pltpu.roll(xs, LANES - 8, 1) * sinm_t)
    return outs[0] if len(outs) == 1 else jnp.concatenate(outs, axis=1)


def _log_sigmoid(x):
    return jnp.minimum(x, 0.0) - jnp.log1p(jnp.exp(-jnp.abs(x)))


def _mixer_in_body(h_ref, g_ref, w_ref, cos_ref, sinp_ref, sinm_ref, kvg_ref, wk_ref, wv_ref,
                   wal_ref, bal_ref,
                   qa_ref, ka_ref, vt_ref, qi_ref, ki_ref, small_ref,
                   bq0_ref, bq1_ref, bq2_ref, bk0_ref, bk1_ref, bk2_ref, bv0_ref, bv1_ref, bv2_ref,
                   cq_ref, ck_ref, cv_ref, cg_ref, la_ref, stage_ref):
    offs, _ = _in_col_offsets()
    u = _rms(h_ref[...], g_ref[...]).astype(BF16)
    cos_t, sinp_t, sinm_t = cos_ref[...], sinp_ref[...], sinm_ref[...]
    tm = h_ref.shape[0]
    gw = B_HEADS_PER_GROUP * HEAD_DIM

    def proj(name):
        lo, hi = offs[name]
        return _dot(u, w_ref[:, lo:hi])

    def emit_dilated(x, out_refs):
        for j in range(x.shape[1] // LANES):
            stage_ref[j] = x[:, j * LANES:(j + 1) * LANES]
        per_g = gw // LANES
        for g, (_, dil) in enumerate(B_GROUPS):
            for r in range(dil):
                rows = [stage_ref[g * per_g + j, pl.ds(r, tm // dil, stride=dil), :] for j in range(per_g)]
                out_refs[g][0, r] = jnp.concatenate(rows, axis=1).astype(BF16)

    rope = lambda x: _rope(x, cos_t, sinp_t, sinm_t)
    q_scale = HEAD_DIM ** -0.5
    qa_ref[...] = (rope(proj("a_q")) * (q_scale * LOG2_E)).astype(BF16)
    qi_ref[...] = (rope(proj("i_q")) * (IDX_DIM ** -0.5)).astype(BF16)
    emit_dilated(rope(proj("b_q")) * q_scale, (bq0_ref, bq1_ref, bq2_ref))
    emit_dilated(rope(proj("b_k")), (bk0_ref, bk1_ref, bk2_ref))
    emit_dilated(proj("b_v"), (bv0_ref, bv1_ref, bv2_ref))
    cq_ref[...] = proj("c_q") * (C_HEAD_K ** -0.5)
    ck_ref[...] = proj("c_k")
    cv_ref[...] = proj("c_v")
    cg_ref[...] = proj("c_g")

    ckv = _rms(proj("ckv"), kvg_ref[...]).astype(BF16)
    ka_ref[...] = rope(_dot(ckv, wk_ref[...])).astype(BF16)
    v_t = _dot_nt(wv_ref[...], ckv)
    slab_row = lax.broadcasted_iota(jnp.int32, v_t.shape, 0) % VT_ROWS
    vt_ref[0] = jnp.where(slab_row >= LANES, 1.0, v_t).astype(BF16)

    small = proj("small")
    small_r = rope(small)
    ki_ref[...] = jnp.where(_half_mask(small.shape, False), small_r,
                            pltpu.roll(small_r, HEAD_DIM, 1)).astype(BF16)
    small_ref[...] = small
    la = _dot(small.astype(BF16), wal_ref[...]) + bal_ref[...]
    la_ref[...] = _log_sigmoid(la) * (1.0 / C_GATE_TAU)


def _mixer_in(h, g, w_big, cos_t, sinp_t, sinm_t, kv_g, w_kup, w_vup_t, w_alpha, b_alpha, *, seq):
    t, d = h.shape
    tm = DSA_KEY_CHUNK
    tiles = seq // tm
    gw = B_HEADS_PER_GROUP * HEAD_DIM
    row = lambda i: (i, 0)
    out_defs = [("q_a", 512, BF16), ("k_a", 512, BF16), ("v_t", "vt", BF16), ("q_i", 512, BF16),
                ("k_i", 128, BF16), ("small", LANES, F32)]
    out_defs += [(f"b_{nm}{gi}", ("dil", dil), BF16) for nm in "qkv" for gi, (_, dil) in enumerate(B_GROUPS)]
    out_defs += [("c_q", 256, F32), ("c_k", 256, F32), ("c_v", 512, F32), ("c_g", 512, F32),
                 ("la", 256, F32)]
    out_specs, out_shape = [], []
    for _, n, dt in out_defs:
        if n == "vt":
            out_specs.append(pl.BlockSpec((1, w_vup_t.shape[0], tm), lambda i: (i, 0, 0)))
            out_shape.append(jax.ShapeDtypeStruct((t // tm, w_vup_t.shape[0], tm), dt))
        elif isinstance(n, tuple):
            dil = n[1]
            out_specs.append(pl.BlockSpec((1, dil, tm // dil, gw),
                                          lambda i: (i // tiles, 0, i % tiles, 0)))
            out_shape.append(jax.ShapeDtypeStruct((t // seq, dil, seq // dil, gw), dt))
        else:
            out_specs.append(pl.BlockSpec((tm, n), row))
            out_shape.append(jax.ShapeDtypeStruct((t, n), dt))
    w_vup = w_vup_t
    outs = pl.pallas_call(
        _mixer_in_body,
        grid=(t // tm,),
        in_specs=[pl.BlockSpec((tm, d), row), _resident((1, d)), _resident(w_big.shape),
                  pl.BlockSpec((tm, LANES), row), pl.BlockSpec((tm, LANES), row),
                  pl.BlockSpec((tm, LANES), row), _resident(kv_g.shape), _resident(w_kup.shape),
                  _resident(w_vup.shape), _resident(w_alpha.shape), _resident(b_alpha.shape)],
        out_specs=out_specs,
        out_shape=out_shape,
        scratch_shapes=[pltpu.VMEM((B_HEADS * HEAD_DIM // LANES, tm, LANES), F32)],
        compiler_params=_params(1),
        name="mixer_in",
    )(h, g, w_big, cos_t, sinp_t, sinm_t, kv_g, w_kup, w_vup, w_alpha, b_alpha)
    return {name: o for (name, _, _), o in zip(out_defs, outs)}


def _masked_pair_rows(x, n_heads):
    out = []
    for h in range(n_heads):
        pair = x[:, (h // 2) * LANES:(h // 2 + 1) * LANES]
        out.append(jnp.where(_half_mask(pair.shape, h % 2 == 1), pair, jnp.zeros_like(pair)))
    return out


def _dsa_body(qi_ref, small_ref, qa_ref, ki_ref, ka_ref, vt_ref, o_ref,
              score_ref, acc_ref, qm_ref, s0_ref, s1_ref, *, seq, top_k):
    kc = DSA_KEY_CHUNK
    qb = Q_BLOCK
    i = pl.program_id(1)
    n_chunks = (i * qb + qb + kc - 1) // kc
    q_pos = i * qb + lax.broadcasted_iota(jnp.int32, (1, qb), 1)
    key_off = lax.broadcasted_iota(jnp.int32, (kc, qb), 0)

    n_pairs = A_HEADS // 2
    qi_heads = _masked_pair_rows(qi_ref[0], IDX_HEADS)
    for pr in range(n_pairs):
        qm_ref[pr] = jnp.concatenate([qi_heads[2 * pr], qi_heads[2 * pr + 1]], axis=0)
    w_t = jnp.transpose(small_ref[0])[SMALL_IW:SMALL_IW + IDX_HEADS, :] * (IDX_HEADS ** -0.5)

    def skewed(first, stage, carry):
        def step(j, st):
            carry, aux0 = st
            aux1 = first(2 * j + 1, s1_ref)
            carry = stage(2 * j, s0_ref, carry, aux0)
            aux0 = first(2 * j + 2, s0_ref)
            return stage(2 * j + 1, s1_ref, carry, aux1), aux0
        return lax.fori_loop(0, (n_chunks + 1) // 2, step, (carry, first(0, s0_ref)))[0]

    def idx_logits_stage(c, s_ref):
        off = pl.multiple_of(jnp.minimum(c, n_chunks - 1) * kc, kc)
        k_c = ki_ref[0, pl.ds(off, kc), :]
        for pr in range(n_pairs):
            s_ref[pr] = _dot_nt(k_c, qm_ref[pr])
        return 0

    def score_stage(c, s_ref, carry, _):
        mx, mn = carry
        cc = jnp.minimum(c, n_chunks - 1)
        sc = jnp.zeros((kc, qb), F32)
        for h in range(IDX_HEADS):
            logit = s_ref[h // 2, :, (h % 2) * qb:(h % 2 + 1) * qb]
            sc = sc + jnp.maximum(logit, 0.0) * w_t[h:h + 1, :]
        causal = (cc * kc + key_off) <= q_pos
        score_ref[cc] = jnp.where(causal, sc, -jnp.inf)
        mx = jnp.maximum(mx, jnp.max(jnp.where(causal, sc, -jnp.inf), axis=0, keepdims=True))
        mn = jnp.minimum(mn, jnp.min(jnp.where(causal, sc, jnp.inf), axis=0, keepdims=True))
        return mx, mn

    row_max, row_min = skewed(
        idx_logits_stage, score_stage,
        (jnp.full((1, qb), -jnp.inf, F32), jnp.full((1, qb), jnp.inf, F32)))

    def count(pred):
        def step(c, acc):
            hit = jnp.where(pred(score_ref[c], c), 1.0, 0.0)
            return acc + jnp.sum(hit.reshape(COUNT_SLABS, kc // COUNT_SLABS, qb), axis=0)
        acc = lax.fori_loop(0, n_chunks, step, jnp.zeros((kc // COUNT_SLABS, qb), F32))
        return jnp.sum(acc, axis=0, keepdims=True)

    k_f = float(top_k)
    n_causal = (q_pos + 1).astype(F32)
    few = n_causal <= k_f
    lo0 = row_min
    hi0 = row_max + jnp.abs(row_max) + 1.0

    def bisect_cond(st):
        return jnp.logical_and(st[0] < BISECT_MAX_ITERS, st[5] > 0.5)

    def bisect_step(st):
        it, lo, hi, c_lo, c_hi, _ = st
        stalled = jnp.zeros((1, qb), jnp.bool_)
        for _ in range(BISECT_UNROLL):
            mid = lo + (hi - lo) * 0.5
            stalled = stalled | (mid <= lo) | (mid >= hi)
            cnt = count(lambda x, c: x >= mid)
            ge = cnt >= k_f
            lo, hi = jnp.where(ge, mid, lo), jnp.where(ge, hi, mid)
            c_lo, c_hi = jnp.where(ge, cnt, c_lo), jnp.where(ge, c_hi, cnt)
        done = few | (c_lo == k_f) | stalled
        return it + BISECT_UNROLL, lo, hi, c_lo, c_hi, jnp.max(jnp.where(done, 0.0, 1.0))

    _, lo, hi, c_lo, c_hi, _ = lax.while_loop(
        bisect_cond, bisect_step,
        (jnp.int32(0), lo0, hi0, n_causal, jnp.zeros((1, qb), F32),
         jnp.max(jnp.where(few, 0.0, 1.0))))

    tied = jnp.logical_and(jnp.logical_not(few), c_lo > k_f)
    need = k_f - c_hi

    def tie_search(_):
        def step(_, st):
            j_lo, j_hi = st
            mid = jnp.floor((j_lo + j_hi) * 0.5)
            cnt = count(lambda x, c: (x >= lo) & (x < hi) & ((c * kc + key_off).astype(F32) <= mid))
            ok = cnt >= need
            return jnp.where(ok, j_lo, mid), jnp.where(ok, mid, j_hi)
        n_steps = max(1, (seq - 1).bit_length()) + 1
        _, j_hi = lax.fori_loop(0, n_steps, step,
                                (jnp.full((1, qb), -1.0, F32), jnp.full((1, qb), float(seq - 1), F32)))
        return j_hi

    any_tied = jnp.max(jnp.where(tied, 1.0, 0.0)) > 0.5
    j_cut = lax.cond(any_tied, tie_search, lambda _: jnp.full((1, qb), float(seq), F32), 0)
    j_cut = jnp.where(tied, j_cut, float(seq))
    thr_hi = jnp.where(tied, hi, lo)

    qa_heads = _masked_pair_rows(qa_ref[0], A_HEADS)
    for pr in range(n_pairs):
        qm_ref[pr] = jnp.concatenate([qa_heads[2 * pr], qa_heads[2 * pr + 1]], axis=0)
    acc_ref[...] = jnp.zeros(acc_ref.shape, F32)

    def attn_logits_stage(c, s_ref):
        cc = jnp.minimum(c, n_chunks - 1)
        off = pl.multiple_of(cc * kc, kc)
        x = score_ref[cc]
        sel = (x >= thr_hi) | ((x >= lo) & ((cc * kc + key_off).astype(F32) <= j_cut))
        bias = jnp.where(sel & (c < n_chunks), 0.0, MASKED_LOGIT)
        col_max = []
        for pr in range(n_pairs):
            s2 = _dot_nt(ka_ref[0, pl.ds(off, kc), pr * LANES:(pr + 1) * LANES], qm_ref[pr])
            for e in range(2):
                s = s2[:, e * qb:(e + 1) * qb] + bias
                s_ref[pr, :, e * qb:(e + 1) * qb] = s
                col_max.append(jnp.max(s, axis=0, keepdims=True))
        return tuple(col_max)

    def softmax_pv_stage(c, s_ref, m_all, col_max):
        cc = jnp.minimum(c, n_chunks - 1)
        m_out = []
        for pr in range(n_pairs):
            p_halves, alphas = [], []
            for e in range(2):
                h = 2 * pr + e
                m_new = jnp.maximum(m_all[h], col_max[h])
                alphas.append(jnp.exp2(m_all[h] - m_new))
                p_halves.append(jnp.exp2(s_ref[pr, :, e * qb:(e + 1) * qb] - m_new).astype(BF16))
                m_out.append(m_new)
            v_c = vt_ref[0, cc, pr * VT_ROWS:(pr + 1) * VT_ROWS, :]
            pv = _dot(v_c, jnp.concatenate(p_halves, axis=1))
            acc_ref[pr] = acc_ref[pr] * jnp.concatenate(alphas, axis=1) + pv
        return tuple(m_out)

    m0 = tuple(jnp.full((1, qb), MASKED_LOGIT, F32) for _ in range(A_HEADS))
    skewed(attn_logits_stage, softmax_pv_stage, m0)

    for pr in range(n_pairs):
        a = acc_ref[pr]
        denom = a[LANES:LANES + 1, :]
        o_t = jnp.concatenate([a[:HEAD_DIM, :qb] / denom[:, :qb],
                               a[HEAD_DIM:LANES, qb:] / denom[:, qb:]], axis=0)
        o_ref[0, :, pr * LANES:(pr + 1) * LANES] = jnp.transpose(o_t).astype(BF16)


def _dsa(q_i, small, q_a, k_i, k_a, v_t, *, top_k):
    bsz, seq, _ = q_a.shape
    qb, kc = Q_BLOCK, DSA_KEY_CHUNK
    blk = lambda b, i: (b, i, 0)
    whole = lambda shape: pl.BlockSpec((1,) + shape, lambda b, i: (b,) + (0,) * len(shape),
                                       pipeline_mode=pl.Buffered(1))
    return pl.pallas_call(
        functools.partial(_dsa_body, seq=seq, top_k=top_k),
        grid=(bsz, seq // qb),
        in_specs=[pl.BlockSpec((1, qb, 512), blk), pl.BlockSpec((1, qb, LANES), blk),
                  pl.BlockSpec((1, qb, 512), blk), whole((seq, LANES)), whole((seq, 512)),
                  whole(v_t.shape[1:])],
        out_specs=pl.BlockSpec((1, qb, 512), blk),
        out_shape=jax.ShapeDtypeStruct((bsz, seq, 512), BF16),
        scratch_shapes=[pltpu.VMEM((seq // kc, kc, qb), F32),
                        pltpu.VMEM((A_HEADS // 2, VT_ROWS, 2 * qb), F32),
                        pltpu.VMEM((A_HEADS // 2, 2 * qb, LANES), BF16),
                        pltpu.VMEM((A_HEADS // 2, kc, 2 * qb), F32),
                        pltpu.VMEM((A_HEADS // 2, kc, 2 * qb), F32)],
        compiler_params=_params(2),
        name="dsa",
    )(q_i, small, q_a, k_i, k_a, v_t)


def _dilated_body(q_ref, kp_ref, k_ref, vp_ref, v_ref, o_ref, lse_ref, *, tq):
    qb = Q_BLOCK
    n = pl.program_id(2)
    row = lax.broadcasted_iota(jnp.int32, (qb, 2 * qb), 0)
    col = lax.broadcasted_iota(jnp.int32, (qb, 2 * qb), 1)
    rel = row + qb - col
    band = (rel >= 0) & (rel <= qb)
    n_pairs = B_HEADS_PER_GROUP // 2
    for s in range(tq // qb):
        q = q_ref[0, 0, s * qb:(s + 1) * qb, :]
        if s == 0:
            k_prev, v_prev = kp_ref[0, 0], vp_ref[0, 0]
            valid = band & ((col >= qb) | (n > 0))
        else:
            k_prev = k_ref[0, 0, (s - 1) * qb:s * qb, :]
            v_prev = v_ref[0, 0, (s - 1) * qb:s * qb, :]
            valid = band
        kk = jnp.concatenate([k_prev, k_ref[0, 0, s * qb:(s + 1) * qb, :]], axis=0)
        vv = jnp.concatenate([v_prev, v_ref[0, 0, s * qb:(s + 1) * qb, :]], axis=0)
        o_parts, lse_parts = [], []
        for pr in range(n_pairs):
            q_pair = q[:, pr * LANES:(pr + 1) * LANES]
            k_pair = kk[:, pr * LANES:(pr + 1) * LANES]
            v_pair = vv[:, pr * LANES:(pr + 1) * LANES]
            o_h, lse_h = [], []
            for odd in (False, True):
                q_h = jnp.where(_half_mask(q_pair.shape, odd), q_pair, jnp.zeros_like(q_pair))
                sc = jnp.where(valid, _dot_nt(q_h, k_pair), -jnp.inf)
                m = jnp.max(sc, axis=1, keepdims=True)
                p = jnp.exp(sc - m)
                l = jnp.sum(p, axis=1, keepdims=True)
                o_h.append(_dot(p.astype(BF16), v_pair) / l)
                lse_h.append(jnp.broadcast_to(m + jnp.log(l), (qb, LANES)))
            lower = _half_mask((qb, LANES), False)
            o_parts.append(jnp.where(lower, o_h[0], o_h[1]))
            lse_parts.append(jnp.where(lower, lse_h[0], lse_h[1]))
        o_ref[0, 0, s * qb:(s + 1) * qb, :] = jnp.concatenate(o_parts, axis=1)
        lse_ref[0, 0, s * qb:(s + 1) * qb, :] = jnp.concatenate(lse_parts, axis=1)


def _dilated_group(b_q, b_k, b_v):
    bsz, dil, sub, gw = b_q.shape
    qb = Q_BLOCK
    tq = min(512, sub)
    cur = lambda b, r, n: (b, r, n, 0)
    prev = lambda b, r, n: (b, r, jnp.maximum(n * (tq // qb) - 1, 0), 0)
    return pl.pallas_call(
        functools.partial(_dilated_body, tq=tq),
        grid=(bsz, dil, sub // tq),
        in_specs=[pl.BlockSpec((1, 1, tq, gw), cur),
                  pl.BlockSpec((1, 1, qb, gw), prev), pl.BlockSpec((1, 1, tq, gw), cur),
                  pl.BlockSpec((1, 1, qb, gw), prev), pl.BlockSpec((1, 1, tq, gw), cur)],
        out_specs=[pl.BlockSpec((1, 1, tq, gw), cur), pl.BlockSpec((1, 1, tq, gw), cur)],
        out_shape=[jax.ShapeDtypeStruct(b_q.shape, F32)] * 2,
        compiler_params=_params(3),
        name=f"dilated_d{dil}",
    )(b_q, b_k, b_k, b_v, b_v)


def _gla_body(q_ref, k_ref, v_ref, g_ref, la_ref, ng_ref, o_ref, st_ref, *, n_sub):
    cs = C_CHUNK

    @pl.when(pl.program_id(1) == 0)
    def _():
        st_ref[...] = jnp.zeros(st_ref.shape, F32)

    r_i = lax.broadcasted_iota(jnp.int32, (cs, cs), 0)
    c_i = lax.broadcasted_iota(jnp.int32, (cs, cs), 1)
    tril = r_i >= c_i
    tri_ones = jnp.where(tril, 1.0, 0.0).astype(BF16)
    ng = ng_ref[...]
    for c in range(n_sub):
        rows = slice(c * cs, (c + 1) * cs)
        la = la_ref[0, rows, :]
        la_hi = la.astype(BF16)
        la_lo = (la - la_hi.astype(F32)).astype(BF16)
        b = _dot(tri_ones, la_hi) + _dot(tri_ones, la_lo)
        b_last = b[cs - 1:cs, :]
        q = q_ref[0, rows, :]
        k = k_ref[0, rows, :]
        q_dec = q * jnp.exp(b)
        k_dec = (k * jnp.exp(-b)).astype(BF16)
        k_end = (k * jnp.exp(b_last - b)).astype(BF16)
        decay = jnp.exp(b_last)
        o_heads = []
        for h in range(C_HEADS):
            p_lo = (h // 2) * LANES
            q_pair = q_dec[:, p_lo:p_lo + LANES]
            q_h = jnp.where(_half_mask(q_pair.shape, h % 2 == 1), q_pair, 0.0).astype(BF16)
            v_h = v_ref[0, rows, h * C_HEAD_V:(h + 1) * C_HEAD_V].astype(BF16)
            attn = jnp.where(tril, _dot_nt(q_h, k_dec[:, p_lo:p_lo + LANES]), 0.0)
            st = st_ref[h]
            o_h = _dot(attn.astype(BF16), v_h) + _dot_nt(q_h, st.astype(BF16))
            st_ref[h] = st * decay[:, p_lo:p_lo + LANES] + _dot_tn(v_h, k_end[:, p_lo:p_lo + LANES])
            gate = g_ref[0, rows, h * C_HEAD_V:(h + 1) * C_HEAD_V]
            o_heads.append(_rms(o_h, ng) * _silu(gate))
        o_ref[0, rows, :] = jnp.concatenate(o_heads, axis=1).astype(BF16)


def _gla(c_q, c_k, c_v, c_g, la, norm_g, *, n_sub=8):
    bsz, seq, _ = c_q.shape
    rows = n_sub * C_CHUNK
    blk = lambda b, n: (b, n, 0)
    kw, vw = C_HEADS * C_HEAD_K, C_HEADS * C_HEAD_V
    return pl.pallas_call(
        functools.partial(_gla_body, n_sub=n_sub),
        grid=(bsz, seq // rows),
        in_specs=[pl.BlockSpec((1, rows, kw), blk), pl.BlockSpec((1, rows, kw), blk),
                  pl.BlockSpec((1, rows, vw), blk), pl.BlockSpec((1, rows, vw), blk),
                  pl.BlockSpec((1, rows, kw), blk), _resident(norm_g.shape)],
        out_specs=pl.BlockSpec((1, rows, vw), blk),
        out_shape=jax.ShapeDtypeStruct((bsz, seq, vw), BF16),
        scratch_shapes=[pltpu.VMEM((C_HEADS, C_HEAD_V, LANES), F32)],
        compiler_params=_params(2),
        name="gla",
    )(c_q, c_k, c_v, c_g, la, norm_g)


def _mixer_out_body(h_ref, ya_ref, ob0_ref, ob1_ref, ob2_ref, ls0_ref, ls1_ref, ls2_ref, yc_ref,
                    pre_g_ref, wg_ref, bg_ref, wa_ref, wb_ref, wc_ref, wo_ref, post_g_ref, o_ref,
                    stage_ref):
    x = h_ref[...]
    tm, d = x.shape
    u = _rms(x, pre_g_ref[...]).astype(BF16)

    def token_order(blk_ref, slot):
        dil = blk_ref.shape[1]
        if dil == 1:
            return blk_ref[0, 0]
        n_l = blk_ref.shape[3] // LANES
        for r in range(dil):
            for j in range(n_l):
                stage_ref[slot * n_l + j, pl.ds(r, tm // dil, stride=dil), :] = (
                    blk_ref[0, r, :, j * LANES:(j + 1) * LANES])
        return jnp.concatenate([stage_ref[slot * n_l + j] for j in range(n_l)], axis=1)

    ls = [token_order(r, j) for j, r in enumerate((ls0_ref, ls1_ref, ls2_ref))]
    ob = [token_order(r, 3 + j) for j, r in enumerate((ob0_ref, ob1_ref, ob2_ref))]
    mx = jnp.maximum(jnp.maximum(ls[0], ls[1]), ls[2])
    ew = [jnp.exp(v - mx) for v in ls]
    y_b = ((ew[0] * ob[0] + ew[1] * ob[1] + ew[2] * ob[2]) / (ew[0] + ew[1] + ew[2])).astype(BF16)

    branches = ((ya_ref[...], wa_ref), (y_b, wb_ref), (yc_ref[...], wc_ref))
    merged = jnp.zeros(x.shape, F32)
    for i, (y, w_ref) in enumerate(branches):
        gate = jax.nn.sigmoid(_dot(u, wg_ref[:, i * d:(i + 1) * d]) + bg_ref[:, i * d:(i + 1) * d])
        merged = merged + gate * _dot(y, w_ref[...])
    m = _dot(merged.astype(BF16), wo_ref[...])
    o_ref[...] = x + _rms(m, post_g_ref[...])


def _mixer_out(h, y_a, o_b, lse_b, y_c, pre_g, w_gate, b_gate, w_a, w_b, w_c, w_o, post_g, *, seq,
               tm=512):
    t, d = h.shape
    tiles = seq // tm
    row = lambda i: (i, 0)

    def tile(a):
        if a.ndim == 2:
            return pl.BlockSpec((tm, a.shape[1]), row)
        dil, gw = a.shape[1], a.shape[3]
        return pl.BlockSpec((1, dil, tm // dil, gw), lambda i: (i // tiles, 0, i % tiles, 0))

    acts = [h, y_a, *o_b, *lse_b, y_c]
    consts = [pre_g, w_gate, b_gate, w_a, w_b, w_c, w_o, post_g]
    return pl.pallas_call(
        _mixer_out_body,
        grid=(t // tm,),
        in_specs=[tile(a) for a in acts] + [_resident(c.shape) for c in consts],
        out_specs=pl.BlockSpec((tm, d), row),
        out_shape=jax.ShapeDtypeStruct((t, d), F32),
        scratch_shapes=[pltpu.VMEM((2 * len(o_b) * o_b[0].shape[3] // LANES, tm, LANES), F32)],
        compiler_params=_params(1),
        name="mixer_out",
    )(*acts, *consts)


def _rope_tables(positions):
    half = ROT_DIM // 2
    inv_freq = ROPE_THETA ** (-jnp.arange(0, ROT_DIM, 2, dtype=F32) / ROT_DIM)
    ang = positions.astype(F32).reshape(-1, 1) * inv_freq
    cos, sin = jnp.cos(ang), jnp.sin(ang)
    t = cos.shape[0]
    pad = lambda n: jnp.zeros((t, n), F32)
    cos_t = jnp.concatenate([cos, cos, jnp.ones((t, HEAD_DIM - 2 * half), F32)], axis=1)
    sinp_t = jnp.concatenate([pad(half), sin, pad(HEAD_DIM - 2 * half)], axis=1)
    sinm_t = jnp.concatenate([-sin, pad(HEAD_DIM - half)], axis=1)
    rep = LANES // HEAD_DIM
    return tuple(jnp.tile(a, (1, rep)) for a in (cos_t, sinp_t, sinm_t))


def _split_w_in(w_in):
    d = w_in.shape[0]
    sizes = (A_HEADS * HEAD_DIM, A_KV_RANK, IDX_HEADS * IDX_DIM, IDX_DIM, IDX_HEADS,
             3 * B_HEADS * HEAD_DIM, C_HEADS * C_HEAD_K, C_HEADS * C_HEAD_K,
             C_HEADS * C_HEAD_V, C_HEADS * C_HEAD_V, C_ALPHA_RANK)
    names = ("a_q", "ckv", "i_q", "i_k", "i_w", "b_qkv", "c_q", "c_k", "c_v", "c_g", "c_a")
    parts, o = {}, 0
    for name, n in zip(names, sizes):
        parts[name] = w_in[:, o:o + n]
        o += n
    bw = B_HEADS * HEAD_DIM
    parts["b_q"], parts["b_k"], parts["b_v"] = (parts["b_qkv"][:, j * bw:(j + 1) * bw] for j in range(3))
    used = IDX_DIM + IDX_HEADS + C_ALPHA_RANK
    parts["small"] = jnp.concatenate(
        [parts["i_k"], parts["i_w"], parts["c_a"], jnp.zeros((d, LANES - used), w_in.dtype)], axis=1)
    return jnp.concatenate([parts[name] for name, _ in _IN_COLS], axis=1).astype(BF16)


def kernel(x, positions, ffn1_pre_g, ffn1_w_in, ffn1_w_out, ffn1_post_g, mix_pre_g, w_in, a_kv_norm_g, a_w_kv_up, c_w_alpha_up, c_b_alpha, c_norm_g, w_branch_a, w_branch_b, w_branch_c, w_gate, b_gate, w_out, mix_post_g, ffn2_pre_g, ffn2_w_in, ffn2_w_out, ffn2_post_g):
    bsz, seq, d = x.shape
    depth = w_in.shape[0]
    top_k = min(TOPK_MAX, seq // 4)
    tables = _rope_tables(positions)
    row = lambda a: a.reshape(1, -1)
    h = x.reshape(bsz * seq, d)
    for l in range(depth):
        h = _ffn(h, row(ffn1_pre_g[l]), ffn1_w_in[l].astype(BF16), ffn1_w_out[l].astype(BF16),
                 row(ffn1_post_g[l]))

        kv_up = a_w_kv_up[l].reshape(A_KV_RANK, A_HEADS, 2, HEAD_DIM)
        w_kup = kv_up[:, :, 0].reshape(A_KV_RANK, A_HEADS * HEAD_DIM).astype(BF16)
        w_vup_t = kv_up[:, :, 1].reshape(A_KV_RANK, A_HEADS // 2, LANES).transpose(1, 2, 0)
        w_vup_t = jnp.pad(w_vup_t, ((0, 0), (0, VT_ROWS - LANES), (0, 0)))
        w_vup_t = w_vup_t.reshape(A_HEADS // 2 * VT_ROWS, A_KV_RANK).astype(BF16)
        w_alpha = jnp.zeros((LANES, C_HEADS * C_HEAD_K), F32)
        w_alpha = w_alpha.at[SMALL_CA:SMALL_CA + C_ALPHA_RANK].set(c_w_alpha_up[l]).astype(BF16)
        p = _mixer_in(h, row(mix_pre_g[l]), _split_w_in(w_in[l]), *tables, row(a_kv_norm_g[l]),
                      w_kup, w_vup_t, w_alpha, row(c_b_alpha[l]), seq=seq)

        seq3 = lambda a: a.reshape(bsz, seq, a.shape[-1])
        v_t = p["v_t"].reshape(bsz, seq // DSA_KEY_CHUNK, -1, DSA_KEY_CHUNK)
        y_a = _dsa(seq3(p["q_i"]), seq3(p["small"]), seq3(p["q_a"]), seq3(p["k_i"]), seq3(p["k_a"]),
                   v_t, top_k=top_k).reshape(bsz * seq, -1)
        o_b, lse_b = [], []
        for g in range(len(B_GROUPS)):
            o_g, lse_g = _dilated_group(p[f"b_q{g}"], p[f"b_k{g}"], p[f"b_v{g}"])
            o_b.append(o_g)
            lse_b.append(lse_g)
        y_c = _gla(seq3(p["c_q"]), seq3(p["c_k"]), seq3(p["c_v"]), seq3(p["c_g"]), seq3(p["la"]),
                   row(c_norm_g[l])).reshape(bsz * seq, -1)

        h = _mixer_out(h, y_a, o_b, lse_b, y_c, row(mix_pre_g[l]), w_gate[l].astype(BF16),
                       row(b_gate[l]), w_branch_a[l].astype(BF16), w_branch_b[l].astype(BF16),
                       w_branch_c[l].astype(BF16), w_out[l].astype(BF16), row(mix_post_g[l]), seq=seq)

        h = _ffn(h, row(ffn2_pre_g[l]), ffn2_w_in[l].astype(BF16), ffn2_w_out[l].astype(BF16),
                 row(ffn2_post_g[l]))
    return h.reshape(bsz, seq, d)
```

```python
import functools

import jax
import jax.numpy as jnp
from jax import lax
from jax.experimental import pallas as pl
from jax.experimental.pallas import tpu as pltpu

F32 = jnp.float32
BF16 = jnp.bfloat16

HEAD_DIM = 64
ROT_DIM = HEAD_DIM // 4
ROPE_THETA = 500000.0
RMS_EPS = 1e-6
MACARON_WEIGHT = 0.5

A_HEADS = 8
A_KV_RANK = 128
IDX_HEADS = 8
IDX_DIM = 64
TOPK_MAX = 256

B_GROUPS = ((128, 1), (512, 4), (2048, 16))
B_HEADS_PER_GROUP = 4
B_HEADS = B_HEADS_PER_GROUP * len(B_GROUPS)

C_HEADS = 4
C_HEAD_K = 64
C_HEAD_V = 128
C_ALPHA_RANK = 16
C_GATE_TAU = 16.0
C_CHUNK = 64

LANES = 128
Q_BLOCK = 128
DSA_KEY_CHUNK = 512
BF16_SUBLANES = 16
VT_ROWS = LANES + BF16_SUBLANES
BISECT_WARMUP = 14
BISECT_UNROLL = 2
BISECT_MAX_ITERS = 16
COUNT_SLABS = 8
MASKED_LOGIT = -1e30
LOG2_E = 1.4426950408889634
VMEM_LIMIT = 60 * 1024 * 1024

SMALL_IK = 0
SMALL_IW = IDX_DIM
SMALL_CA = IDX_DIM + IDX_HEADS


def _dot(a, b):
    return jnp.dot(a, b, preferred_element_type=F32)


def _dot_nt(a, b):
    return lax.dot_general(a, b, (((1,), (1,)), ((), ())), preferred_element_type=F32)


def _dot_tn(a, b):
    return lax.dot_general(a, b, (((0,), (0,)), ((), ())), preferred_element_type=F32)


def _rms(x, g):
    return x * lax.rsqrt(jnp.mean(x * x, axis=-1, keepdims=True) + RMS_EPS) * g


def _silu(x):
    return x * jax.nn.sigmoid(x)


def _params(n_grid):
    return pltpu.CompilerParams(dimension_semantics=("arbitrary",) * n_grid,
                                vmem_limit_bytes=VMEM_LIMIT)


def _resident(shape):
    zeros = (0,) * len(shape)
    return pl.BlockSpec(shape, lambda *_: zeros, pipeline_mode=pl.Buffered(1))


def _half_mask(shape, upper):
    lane = lax.broadcasted_iota(jnp.int32, shape, len(shape) - 1) % LANES
    return (lane >= HEAD_DIM) if upper else (lane < HEAD_DIM)


def _ffn_body(h_ref, pre_g_ref, w_in_ref, w_out_ref, post_g_ref, o_ref, *, d_ff, tf):
    x = h_ref[...]
    xn = _rms(x, pre_g_ref[...]).astype(BF16)
    acc = jnp.zeros(x.shape, F32)
    for j in range(d_ff // tf):
        gate = _dot(xn, w_in_ref[:, j * tf:(j + 1) * tf])
        up = _dot(xn, w_in_ref[:, d_ff + j * tf:d_ff + (j + 1) * tf])
        act = (_silu(gate) * up).astype(BF16)
        acc = acc + _dot(act, w_out_ref[j * tf:(j + 1) * tf, :])
    o_ref[...] = x + MACARON_WEIGHT * _rms(acc, post_g_ref[...])


def _ffn(h, pre_g, w_in, w_out, post_g, *, tm=512):
    t, d = h.shape
    d_ff = w_out.shape[0]
    tf = d_ff // 2 if (d_ff // 2) % LANES == 0 else d_ff
    row = lambda i: (i, 0)
    return pl.pallas_call(
        functools.partial(_ffn_body, d_ff=d_ff, tf=tf),
        grid=(t // tm,),
        in_specs=[pl.BlockSpec((tm, d), row), _resident((1, d)), _resident(w_in.shape),
                  _resident(w_out.shape), _resident((1, d))],
        out_specs=pl.BlockSpec((tm, d), row),
        out_shape=jax.ShapeDtypeStruct((t, d), F32),
        compiler_params=_params(1),
        name="ffn",
    )(h, pre_g, w_in, w_out, post_g)


_IN_COLS = (("a_q", 512), ("i_q", 512), ("b_q", 768), ("b_k", 768), ("b_v", 768),
            ("c_q", 256), ("c_k", 256), ("c_v", 512), ("c_g", 512), ("ckv", 128), ("small", 128))


def _in_col_offsets():
    offs, o = {}, 0
    for name, n in _IN_COLS:
        offs[name] = (o, o + n)
        o += n
    return offs, o


def _rope(x, cos_t, sinp_t, sinm_t):
    outs = []
    for i in range(x.shape[1] // LANES):
        xs = x[:, i * LANES:(i + 1) * LANES]
        outs.append(xs * cos_t + pltpu.roll(xs, 8, 1) * sinp_t + pltpu.roll(xs, LANES - 8, 1) * sinm_t)
    return outs[0] if len(outs) == 1 else jnp.concatenate(outs, axis=1)


def _log_sigmoid(x):
    return jnp.minimum(x, 0.0) - jnp.log1p(jnp.exp(-jnp.abs(x)))


def _mixer_in_body(h_ref, g_ref, w_ref, cos_ref, sinp_ref, sinm_ref, kvg_ref, wk_ref, wv_ref,
                   wal_ref, bal_ref,
                   qa_ref, ka_ref, vt_ref, qi_ref, ki_ref, small_ref,
                   bq0_ref, bq1_ref, bq2_ref, bk0_ref, bk1_ref, bk2_ref, bv0_ref, bv1_ref, bv2_ref,
                   cq_ref, ck_ref, cv_ref, cg_ref, la_ref, stage_ref):
    offs, _ = _in_col_offsets()
    u = _rms(h_ref[...], g_ref[...]).astype(BF16)
    cos_t, sinp_t, sinm_t = cos_ref[...], sinp_ref[...], sinm_ref[...]
    tm = h_ref.shape[0]
    gw = B_HEADS_PER_GROUP * HEAD_DIM

    def proj(name):
        lo, hi = offs[name]
        return _dot(u, w_ref[:, lo:hi])

    def emit_dilated(x, out_refs):
        for j in range(x.shape[1] // LANES):
            stage_ref[j] = x[:, j * LANES:(j + 1) * LANES]
        per_g = gw // LANES
        for g, (_, dil) in enumerate(B_GROUPS):
            for r in range(dil):
                rows = [stage_ref[g * per_g + j, pl.ds(r, tm // dil, stride=dil), :] for j in range(per_g)]
                out_refs[g][0, r] = jnp.concatenate(rows, axis=1).astype(BF16)

    rope = lambda x: _rope(x, cos_t, sinp_t, sinm_t)
    q_scale = HEAD_DIM ** -0.5
    qa_ref[...] = (rope(proj("a_q")) * (q_scale * LOG2_E)).astype(BF16)
    qi_ref[...] = (rope(proj("i_q")) * (IDX_DIM ** -0.5)).astype(BF16)
    emit_dilated(rope(proj("b_q")) * q_scale, (bq0_ref, bq1_ref, bq2_ref))
    emit_dilated(rope(proj("b_k")), (bk0_ref, bk1_ref, bk2_ref))
    emit_dilated(proj("b_v"), (bv0_ref, bv1_ref, bv2_ref))
    cq_ref[...] = proj("c_q") * (C_HEAD_K ** -0.5)
    ck_ref[...] = proj("c_k")
    cv_ref[...] = proj("c_v")
    cg_ref[...] = proj("c_g")

    ckv = _rms(proj("ckv"), kvg_ref[...]).astype(BF16)
    ka_ref[...] = rope(_dot(ckv, wk_ref[...])).astype(BF16)
    v_t = _dot_nt(wv_ref[...], ckv)
    slab_row = lax.broadcasted_iota(jnp.int32, v_t.shape, 0) % VT_ROWS
    vt_ref[0] = jnp.where(slab_row >= LANES, 1.0, v_t).astype(BF16)

    small = proj("small")
    small_r = rope(small)
    ki_ref[...] = jnp.where(_half_mask(small.shape, False), small_r,
                            pltpu.roll(small_r, HEAD_DIM, 1)).astype(BF16)
    small_ref[...] = small
    la = _dot(small.astype(BF16), wal_ref[...]) + bal_ref[...]
    la_ref[...] = _log_sigmoid(la) * (1.0 / C_GATE_TAU)


def _mixer_in(h, g, w_big, cos_t, sinp_t, sinm_t, kv_g, w_kup, w_vup_t, w_alpha, b_alpha, *, seq):
    t, d = h.shape
    tm = DSA_KEY_CHUNK
    tiles = seq // tm
    gw = B_HEADS_PER_GROUP * HEAD_DIM
    row = lambda i: (i, 0)
    out_defs = [("q_a", 512, BF16), ("k_a", 512, BF16), ("v_t", "vt", BF16), ("q_i", 512, BF16),
                ("k_i", 128, BF16), ("small", LANES, F32)]
    out_defs += [(f"b_{nm}{gi}", ("dil", dil), BF16) for nm in "qkv" for gi, (_, dil) in enumerate(B_GROUPS)]
    out_defs += [("c_q", 256, F32), ("c_k", 256, F32), ("c_v", 512, F32), ("c_g", 512, F32),
                 ("la", 256, F32)]
    out_specs, out_shape = [], []
    for _, n, dt in out_defs:
        if n == "vt":
            out_specs.append(pl.BlockSpec((1, w_vup_t.shape[0], tm), lambda i: (i, 0, 0)))
            out_shape.append(jax.ShapeDtypeStruct((t // tm, w_vup_t.shape[0], tm), dt))
        elif isinstance(n, tuple):
            dil = n[1]
            out_specs.append(pl.BlockSpec((1, dil, tm // dil, gw),
                                          lambda i: (i // tiles, 0, i % tiles, 0)))
            out_shape.append(jax.ShapeDtypeStruct((t // seq, dil, seq // dil, gw), dt))
        else:
            out_specs.append(pl.BlockSpec((tm, n), row))
            out_shape.append(jax.ShapeDtypeStruct((t, n), dt))
    w_vup = w_vup_t
    outs = pl.pallas_call(
        _mixer_in_body,
        grid=(t // tm,),
        in_specs=[pl.BlockSpec((tm, d), row), _resident((1, d)), _resident(w_big.shape),
                  pl.BlockSpec((tm, LANES), row), pl.BlockSpec((tm, LANES), row),
                  pl.BlockSpec((tm, LANES), row), _resident(kv_g.shape), _resident(w_kup.shape),
                  _resident(w_vup.shape), _resident(w_alpha.shape), _resident(b_alpha.shape)],
        out_specs=out_specs,
        out_shape=out_shape,
        scratch_shapes=[pltpu.VMEM((B_HEADS * HEAD_DIM // LANES, tm, LANES), F32)],
        compiler_params=_params(1),
        name="mixer_in",
    )(h, g, w_big, cos_t, sinp_t, sinm_t, kv_g, w_kup, w_vup, w_alpha, b_alpha)
    return {name: o for (name, _, _), o in zip(out_defs, outs)}


def _masked_pair_rows(x, n_heads):
    out = []
    for h in range(n_heads):
        pair = x[:, (h // 2) * LANES:(h // 2 + 1) * LANES]
        out.append(jnp.where(_half_mask(pair.shape, h % 2 == 1), pair, jnp.zeros_like(pair)))
    return out


def _dsa_body(qi_ref, small_ref, qa_ref, ki_ref, ka_ref, vt_ref, o_ref,
              score_ref, acc_ref, qm_ref, s0_ref, s1_ref, *, seq, top_k):
    kc = DSA_KEY_CHUNK
    qb = Q_BLOCK
    i = pl.program_id(1)
    n_chunks = (i * qb + qb + kc - 1) // kc
    q_pos = i * qb + lax.broadcasted_iota(jnp.int32, (1, qb), 1)
    key_off = lax.broadcasted_iota(jnp.int32, (kc, qb), 0)

    n_pairs = A_HEADS // 2
    qi_heads = _masked_pair_rows(qi_ref[0], IDX_HEADS)
    for pr in range(n_pairs):
        qm_ref[pr] = jnp.concatenate([qi_heads[2 * pr], qi_heads[2 * pr + 1]], axis=0)
    w_t = jnp.transpose(small_ref[0])[SMALL_IW:SMALL_IW + IDX_HEADS, :] * (IDX_HEADS ** -0.5)

    def skewed(first, stage, carry):
        def step(j, st):
            carry, aux0 = st
            aux1 = first(2 * j + 1, s1_ref)
            carry = stage(2 * j, s0_ref, carry, aux0)
            aux0 = first(2 * j + 2, s0_ref)
            return stage(2 * j + 1, s1_ref, carry, aux1), aux0
        return lax.fori_loop(0, (n_chunks + 1) // 2, step, (carry, first(0, s0_ref)))[0]

    def idx_logits_stage(c, s_ref):
        off = pl.multiple_of(jnp.minimum(c, n_chunks - 1) * kc, kc)
        k_c = ki_ref[0, pl.ds(off, kc), :]
        for pr in range(n_pairs):
            s_ref[pr] = _dot_nt(k_c, qm_ref[pr])
        return 0

    def score_stage(c, s_ref, carry, _):
        mx, mn = carry
        cc = jnp.minimum(c, n_chunks - 1)
        sc = jnp.zeros((kc, qb), F32)
        for h in range(IDX_HEADS):
            logit = s_ref[h // 2, :, (h % 2) * qb:(h % 2 + 1) * qb]
            sc = sc + jnp.maximum(logit, 0.0) * w_t[h:h + 1, :]
        causal = (cc * kc + key_off) <= q_pos
        score_ref[cc] = jnp.where(causal, sc, -jnp.inf)
        mx = jnp.maximum(mx, jnp.max(jnp.where(causal, sc, -jnp.inf), axis=0, keepdims=True))
        mn = jnp.minimum(mn, jnp.min(jnp.where(causal, sc, jnp.inf), axis=0, keepdims=True))
        return mx, mn

    row_max, row_min = skewed(
        idx_logits_stage, score_stage,
        (jnp.full((1, qb), -jnp.inf, F32), jnp.full((1, qb), jnp.inf, F32)))

    slab = kc // COUNT_SLABS

    def sweep(init, fold, per_chunk):
        def step(c, acc):
            return fold(acc, per_chunk(score_ref[c]).reshape(COUNT_SLABS, slab, qb))
        return lax.fori_loop(0, n_chunks, step, jnp.full((slab, qb), init, F32))

    def count_ge(thr):
        acc = sweep(0.0, lambda a, v: a + jnp.sum(v, axis=0), lambda x: jnp.where(x >= thr, 1.0, 0.0))
        return jnp.sum(acc, axis=0, keepdims=True)

    def max_below(thr):
        acc = sweep(-jnp.inf, lambda a, v: jnp.maximum(a, jnp.max(v, axis=0)),
                    lambda x: jnp.where(x < thr, x, -jnp.inf))
        return jnp.max(acc, axis=0, keepdims=True)

    k_f = float(top_k)
    n_causal = (q_pos + 1).astype(F32)
    few = n_causal <= k_f
    lo0 = row_min
    hi0 = row_max + jnp.abs(row_max) + 1.0

    def halve(st):
        lo, hi, c_lo, c_hi = st
        mid = lo + (hi - lo) * 0.5
        cnt = count_ge(mid)
        ge = cnt >= k_f
        return (jnp.where(ge, mid, lo), jnp.where(ge, hi, mid),
                jnp.where(ge, cnt, c_lo), jnp.where(ge, c_hi, cnt))

    def snap(st):
        lo, hi, c_lo, c_hi = st
        v = max_below(hi)
        cnt = count_ge(v)
        ge = jnp.logical_and(cnt >= k_f, jnp.logical_not(few))
        lower = jnp.logical_and(cnt < k_f, jnp.logical_not(few))
        return (jnp.where(ge, v, lo), jnp.where(lower, v, hi),
                jnp.where(ge, cnt, c_lo), jnp.where(lower, cnt, c_hi)), ge

    st0 = (lo0, hi0, n_causal, jnp.zeros((1, qb), F32))
    st1 = lax.fori_loop(0, BISECT_WARMUP, lambda _, st: halve(st), st0)

    def search_cond(carry):
        return jnp.logical_and(carry[0] < BISECT_MAX_ITERS, carry[2] > 0.5)

    def search_step(carry):
        it, st, _ = carry
        for _ in range(BISECT_UNROLL):
            st = halve(st)
        st, snapped = snap(st)
        done = few | (st[2] == k_f) | snapped
        return it + 1, st, jnp.max(jnp.where(done, 0.0, 1.0))

    pending0 = jnp.max(jnp.where(few | (st1[2] == k_f), 0.0, 1.0))
    _, (lo, hi, c_lo, c_hi), _ = lax.while_loop(search_cond, search_step, (jnp.int32(0), st1, pending0))

    tied = jnp.logical_and(jnp.logical_not(few), c_lo > k_f)
    need = k_f - c_hi

    def tie_cut(_):
        r_i = lax.broadcasted_iota(jnp.int32, (kc, kc), 0)
        c_i = lax.broadcasted_iota(jnp.int32, (kc, kc), 1)
        tri = jnp.where(r_i >= c_i, 1.0, 0.0).astype(BF16)

        def step(c, st):
            seen, j_best = st
            x = score_ref[c]
            tie = (x >= lo) & (x < hi)
            prefix = seen + _dot(tri, jnp.where(tie, 1.0, 0.0).astype(BF16))
            idx = (c * kc + key_off).astype(F32)
            kept = jnp.where(tie & (prefix <= need), idx, -1.0)
            return prefix[kc - 1:kc, :], jnp.maximum(j_best, jnp.max(kept, axis=0, keepdims=True))
        return lax.fori_loop(0, n_chunks, step,
                             (jnp.zeros((1, qb), F32), jnp.full((1, qb), -1.0, F32)))[1]

    any_tied = jnp.max(jnp.where(tied, 1.0, 0.0)) > 0.5
    j_cut = lax.cond(any_tied, tie_cut, lambda _: jnp.full((1, qb), float(seq), F32), 0)
    j_cut = jnp.where(tied, j_cut, float(seq))
    thr_hi = jnp.where(tied, hi, lo)

    qa_heads = _masked_pair_rows(qa_ref[0], A_HEADS)
    for pr in range(n_pairs):
        qm_ref[pr] = jnp.concatenate([qa_heads[2 * pr], qa_heads[2 * pr + 1]], axis=0)
    acc_ref[...] = jnp.zeros(acc_ref.shape, F32)

    def attn_logits_stage(c, s_ref):
        cc = jnp.minimum(c, n_chunks - 1)
        off = pl.multiple_of(cc * kc, kc)
        x = score_ref[cc]
        sel = (x >= thr_hi) | ((x >= lo) & ((cc * kc + key_off).astype(F32) <= j_cut))
        bias = jnp.where(sel & (c < n_chunks), 0.0, MASKED_LOGIT)
        col_max = []
        for pr in range(n_pairs):
            s2 = _dot_nt(ka_ref[0, pl.ds(off, kc), pr * LANES:(pr + 1) * LANES], qm_ref[pr])
            for e in range(2):
                s = s2[:, e * qb:(e + 1) * qb] + bias
                s_ref[pr, :, e * qb:(e + 1) * qb] = s
                col_max.append(jnp.max(s, axis=0, keepdims=True))
        return tuple(col_max)

    def softmax_pv_stage(c, s_ref, m_all, col_max):
        cc = jnp.minimum(c, n_chunks - 1)
        m_out = []
        for pr in range(n_pairs):
            p_halves, alphas = [], []
            for e in range(2):
                h = 2 * pr + e
                m_new = jnp.maximum(m_all[h], col_max[h])
                alphas.append(jnp.exp2(m_all[h] - m_new))
                p_halves.append(jnp.exp2(s_ref[pr, :, e * qb:(e + 1) * qb] - m_new).astype(BF16))
                m_out.append(m_new)
            v_c = vt_ref[0, cc, pr * VT_ROWS:(pr + 1) * VT_ROWS, :]
            pv = _dot(v_c, jnp.concatenate(p_halves, axis=1))
            acc_ref[pr] = acc_ref[pr] * jnp.concatenate(alphas, axis=1) + pv
        return tuple(m_out)

    m0 = tuple(jnp.full((1, qb), MASKED_LOGIT, F32) for _ in range(A_HEADS))
    skewed(attn_logits_stage, softmax_pv_stage, m0)

    for pr in range(n_pairs):
        a = acc_ref[pr]
        denom = a[LANES:LANES + 1, :]
        o_t = jnp.concatenate([a[:HEAD_DIM, :qb] / denom[:, :qb],
                               a[HEAD_DIM:LANES, qb:] / denom[:, qb:]], axis=0)
        o_ref[0, :, pr * LANES:(pr + 1) * LANES] = jnp.transpose(o_t).astype(BF16)


def _dsa(q_i, small, q_a, k_i, k_a, v_t, *, top_k):
    bsz, seq, _ = q_a.shape
    qb, kc = Q_BLOCK, DSA_KEY_CHUNK
    blk = lambda b, i: (b, i, 0)
    whole = lambda shape: pl.BlockSpec((1,) + shape, lambda b, i: (b,) + (0,) * len(shape),
                                       pipeline_mode=pl.Buffered(1))
    return pl.pallas_call(
        functools.partial(_dsa_body, seq=seq, top_k=top_k),
        grid=(bsz, seq // qb),
        in_specs=[pl.BlockSpec((1, qb, 512), blk), pl.BlockSpec((1, qb, LANES), blk),
                  pl.BlockSpec((1, qb, 512), blk), whole((seq, LANES)), whole((seq, 512)),
                  whole(v_t.shape[1:])],
        out_specs=pl.BlockSpec((1, qb, 512), blk),
        out_shape=jax.ShapeDtypeStruct((bsz, seq, 512), BF16),
        scratch_shapes=[pltpu.VMEM((seq // kc, kc, qb), F32),
                        pltpu.VMEM((A_HEADS // 2, VT_ROWS, 2 * qb), F32),
                        pltpu.VMEM((A_HEADS // 2, 2 * qb, LANES), BF16),
                        pltpu.VMEM((A_HEADS // 2, kc, 2 * qb), F32),
                        pltpu.VMEM((A_HEADS // 2, kc, 2 * qb), F32)],
        compiler_params=_params(2),
        name="dsa",
    )(q_i, small, q_a, k_i, k_a, v_t)


def _dilated_body(q_ref, kp_ref, k_ref, vp_ref, v_ref, o_ref, lse_ref, *, tq):
    qb = Q_BLOCK
    n = pl.program_id(2)
    row = lax.broadcasted_iota(jnp.int32, (qb, 2 * qb), 0)
    col = lax.broadcasted_iota(jnp.int32, (qb, 2 * qb), 1)
    rel = row + qb - col
    band = (rel >= 0) & (rel <= qb)
    n_pairs = B_HEADS_PER_GROUP // 2
    for s in range(tq // qb):
        q = q_ref[0, 0, s * qb:(s + 1) * qb, :]
        if s == 0:
            k_prev, v_prev = kp_ref[0, 0], vp_ref[0, 0]
            valid = band & ((col >= qb) | (n > 0))
        else:
            k_prev = k_ref[0, 0, (s - 1) * qb:s * qb, :]
            v_prev = v_ref[0, 0, (s - 1) * qb:s * qb, :]
            valid = band
        kk = jnp.concatenate([k_prev, k_ref[0, 0, s * qb:(s + 1) * qb, :]], axis=0)
        vv = jnp.concatenate([v_prev, v_ref[0, 0, s * qb:(s + 1) * qb, :]], axis=0)
        o_parts, lse_parts = [], []
        for pr in range(n_pairs):
            q_pair = q[:, pr * LANES:(pr + 1) * LANES]
            k_pair = kk[:, pr * LANES:(pr + 1) * LANES]
            v_pair = vv[:, pr * LANES:(pr + 1) * LANES]
            o_h, lse_h = [], []
            for odd in (False, True):
                q_h = jnp.where(_half_mask(q_pair.shape, odd), q_pair, jnp.zeros_like(q_pair))
                sc = jnp.where(valid, _dot_nt(q_h, k_pair), -jnp.inf)
                m = jnp.max(sc, axis=1, keepdims=True)
                p = jnp.exp(sc - m)
                l = jnp.sum(p, axis=1, keepdims=True)
                o_h.append(_dot(p.astype(BF16), v_pair) / l)
                lse_h.append(jnp.broadcast_to(m + jnp.log(l), (qb, LANES)))
            lower = _half_mask((qb, LANES), False)
            o_parts.append(jnp.where(lower, o_h[0], o_h[1]))
            lse_parts.append(jnp.where(lower, lse_h[0], lse_h[1]))
        o_ref[0, 0, s * qb:(s + 1) * qb, :] = jnp.concatenate(o_parts, axis=1)
        lse_ref[0, 0, s * qb:(s + 1) * qb, :] = jnp.concatenate(lse_parts, axis=1)


def _dilated_group(b_q, b_k, b_v):
    bsz, dil, sub, gw = b_q.shape
    qb = Q_BLOCK
    tq = min(512, sub)
    cur = lambda b, r, n: (b, r, n, 0)
    prev = lambda b, r, n: (b, r, jnp.maximum(n * (tq // qb) - 1, 0), 0)
    return pl.pallas_call(
        functools.partial(_dilated_body, tq=tq),
        grid=(bsz, dil, sub // tq),
        in_specs=[pl.BlockSpec((1, 1, tq, gw), cur),
                  pl.BlockSpec((1, 1, qb, gw), prev), pl.BlockSpec((1, 1, tq, gw), cur),
                  pl.BlockSpec((1, 1, qb, gw), prev), pl.BlockSpec((1, 1, tq, gw), cur)],
        out_specs=[pl.BlockSpec((1, 1, tq, gw), cur), pl.BlockSpec((1, 1, tq, gw), cur)],
        out_shape=[jax.ShapeDtypeStruct(b_q.shape, F32)] * 2,
        compiler_params=_params(3),
        name=f"dilated_d{dil}",
    )(b_q, b_k, b_k, b_v, b_v)


def _gla_body(q_ref, k_ref, v_ref, g_ref, la_ref, ng_ref, o_ref, st_ref, *, n_sub):
    cs = C_CHUNK
    bsz = q_ref.shape[0]

    @pl.when(pl.program_id(0) == 0)
    def _():
        st_ref[...] = jnp.zeros(st_ref.shape, F32)

    r_i = lax.broadcasted_iota(jnp.int32, (cs, cs), 0)
    c_i = lax.broadcasted_iota(jnp.int32, (cs, cs), 1)
    tril = r_i >= c_i
    tri_ones = jnp.where(tril, 1.0, 0.0).astype(BF16)
    ng = ng_ref[...]
    for c in range(n_sub):
        rows = slice(c * cs, (c + 1) * cs)
        for bi in range(bsz):
            la = la_ref[bi, rows, :]
            la_hi = la.astype(BF16)
            la_lo = (la - la_hi.astype(F32)).astype(BF16)
            b = _dot(tri_ones, la_hi) + _dot(tri_ones, la_lo)
            b_last = b[cs - 1:cs, :]
            q = q_ref[bi, rows, :]
            k = k_ref[bi, rows, :]
            q_dec = q * jnp.exp(b)
            k_dec = (k * jnp.exp(-b)).astype(BF16)
            k_end = (k * jnp.exp(b_last - b)).astype(BF16)
            decay = jnp.exp(b_last)
            o_heads = []
            for h in range(C_HEADS):
                p_lo = (h // 2) * LANES
                q_pair = q_dec[:, p_lo:p_lo + LANES]
                q_h = jnp.where(_half_mask(q_pair.shape, h % 2 == 1), q_pair, 0.0).astype(BF16)
                v_h = v_ref[bi, rows, h * C_HEAD_V:(h + 1) * C_HEAD_V].astype(BF16)
                attn = jnp.where(tril, _dot_nt(q_h, k_dec[:, p_lo:p_lo + LANES]), 0.0)
                st = st_ref[bi * C_HEADS + h]
                o_h = _dot(attn.astype(BF16), v_h) + _dot_nt(q_h, st.astype(BF16))
                st_ref[bi * C_HEADS + h] = (st * decay[:, p_lo:p_lo + LANES]
                                            + _dot_tn(v_h, k_end[:, p_lo:p_lo + LANES]))
                gate = g_ref[bi, rows, h * C_HEAD_V:(h + 1) * C_HEAD_V]
                o_heads.append(_rms(o_h, ng) * _silu(gate))
            o_ref[bi, rows, :] = jnp.concatenate(o_heads, axis=1).astype(BF16)


def _gla(c_q, c_k, c_v, c_g, la, norm_g, *, n_sub=8):
    bsz, seq, _ = c_q.shape
    rows = n_sub * C_CHUNK
    blk = lambda n: (0, n, 0)
    kw, vw = C_HEADS * C_HEAD_K, C_HEADS * C_HEAD_V
    return pl.pallas_call(
        functools.partial(_gla_body, n_sub=n_sub),
        grid=(seq // rows,),
        in_specs=[pl.BlockSpec((bsz, rows, kw), blk), pl.BlockSpec((bsz, rows, kw), blk),
                  pl.BlockSpec((bsz, rows, vw), blk), pl.BlockSpec((bsz, rows, vw), blk),
                  pl.BlockSpec((bsz, rows, kw), blk), _resident(norm_g.shape)],
        out_specs=pl.BlockSpec((bsz, rows, vw), blk),
        out_shape=jax.ShapeDtypeStruct((bsz, seq, vw), BF16),
        scratch_shapes=[pltpu.VMEM((bsz * C_HEADS, C_HEAD_V, LANES), F32)],
        compiler_params=_params(1),
        name="gla",
    )(c_q, c_k, c_v, c_g, la, norm_g)


def _mixer_out_body(h_ref, ya_ref, ob0_ref, ob1_ref, ob2_ref, ls0_ref, ls1_ref, ls2_ref, yc_ref,
                    pre_g_ref, wg_ref, bg_ref, wa_ref, wb_ref, wc_ref, wo_ref, post_g_ref, o_ref,
                    stage_ref):
    x = h_ref[...]
    tm, d = x.shape
    u = _rms(x, pre_g_ref[...]).astype(BF16)

    def token_order(blk_ref, slot):
        dil = blk_ref.shape[1]
        if dil == 1:
            return blk_ref[0, 0]
        n_l = blk_ref.shape[3] // LANES
        for r in range(dil):
            for j in range(n_l):
                stage_ref[slot * n_l + j, pl.ds(r, tm // dil, stride=dil), :] = (
                    blk_ref[0, r, :, j * LANES:(j + 1) * LANES])
        return jnp.concatenate([stage_ref[slot * n_l + j] for j in range(n_l)], axis=1)

    ls = [token_order(r, j) for j, r in enumerate((ls0_ref, ls1_ref, ls2_ref))]
    ob = [token_order(r, 3 + j) for j, r in enumerate((ob0_ref, ob1_ref, ob2_ref))]
    mx = jnp.maximum(jnp.maximum(ls[0], ls[1]), ls[2])
    ew = [jnp.exp(v - mx) for v in ls]
    y_b = ((ew[0] * ob[0] + ew[1] * ob[1] + ew[2] * ob[2]) / (ew[0] + ew[1] + ew[2])).astype(BF16)

    branches = ((ya_ref[...], wa_ref), (y_b, wb_ref), (yc_ref[...], wc_ref))
    merged = jnp.zeros(x.shape, F32)
    for i, (y, w_ref) in enumerate(branches):
        gate = jax.nn.sigmoid(_dot(u, wg_ref[:, i * d:(i + 1) * d]) + bg_ref[:, i * d:(i + 1) * d])
        merged = merged + gate * _dot(y, w_ref[...])
    m = _dot(merged.astype(BF16), wo_ref[...])
    o_ref[...] = x + _rms(m, post_g_ref[...])


def _mixer_out(h, y_a, o_b, lse_b, y_c, pre_g, w_gate, b_gate, w_a, w_b, w_c, w_o, post_g, *, seq,
               tm=512):
    t, d = h.shape
    tiles = seq // tm
    row = lambda i: (i, 0)

    def tile(a):
        if a.ndim == 2:
            return pl.BlockSpec((tm, a.shape[1]), row)
        dil, gw = a.shape[1], a.shape[3]
        return pl.BlockSpec((1, dil, tm // dil, gw), lambda i: (i // tiles, 0, i % tiles, 0))

    acts = [h, y_a, *o_b, *lse_b, y_c]
    consts = [pre_g, w_gate, b_gate, w_a, w_b, w_c, w_o, post_g]
    return pl.pallas_call(
        _mixer_out_body,
        grid=(t // tm,),
        in_specs=[tile(a) for a in acts] + [_resident(c.shape) for c in consts],
        out_specs=pl.BlockSpec((tm, d), row),
        out_shape=jax.ShapeDtypeStruct((t, d), F32),
        scratch_shapes=[pltpu.VMEM((2 * len(o_b) * o_b[0].shape[3] // LANES, tm, LANES), F32)],
        compiler_params=_params(1),
        name="mixer_out",
    )(*acts, *consts)


def _rope_tables(positions):
    half = ROT_DIM // 2
    inv_freq = ROPE_THETA ** (-jnp.arange(0, ROT_DIM, 2, dtype=F32) / ROT_DIM)
    ang = positions.astype(F32).reshape(-1, 1) * inv_freq
    cos, sin = jnp.cos(ang), jnp.sin(ang)
    t = cos.shape[0]
    pad = lambda n: jnp.zeros((t, n), F32)
    cos_t = jnp.concatenate([cos, cos, jnp.ones((t, HEAD_DIM - 2 * half), F32)], axis=1)
    sinp_t = jnp.concatenate([pad(half), sin, pad(HEAD_DIM - 2 * half)], axis=1)
    sinm_t = jnp.concatenate([-sin, pad(HEAD_DIM - half)], axis=1)
    rep = LANES // HEAD_DIM
    return tuple(jnp.tile(a, (1, rep)) for a in (cos_t, sinp_t, sinm_t))


def _split_w_in(w_in):
    d = w_in.shape[0]
    sizes = (A_HEADS * HEAD_DIM, A_KV_RANK, IDX_HEADS * IDX_DIM, IDX_DIM, IDX_HEADS,
             3 * B_HEADS * HEAD_DIM, C_HEADS * C_HEAD_K, C_HEADS * C_HEAD_K,
             C_HEADS * C_HEAD_V, C_HEADS * C_HEAD_V, C_ALPHA_RANK)
    names = ("a_q", "ckv", "i_q", "i_k", "i_w", "b_qkv", "c_q", "c_k", "c_v", "c_g", "c_a")
    parts, o = {}, 0
    for name, n in zip(names, sizes):
        parts[name] = w_in[:, o:o + n]
        o += n
    bw = B_HEADS * HEAD_DIM
    parts["b_q"], parts["b_k"], parts["b_v"] = (parts["b_qkv"][:, j * bw:(j + 1) * bw] for j in range(3))
    used = IDX_DIM + IDX_HEADS + C_ALPHA_RANK
    parts["small"] = jnp.concatenate(
        [parts["i_k"], parts["i_w"], parts["c_a"], jnp.zeros((d, LANES - used), w_in.dtype)], axis=1)
    return jnp.concatenate([parts[name] for name, _ in _IN_COLS], axis=1).astype(BF16)


def kernel(x, positions, ffn1_pre_g, ffn1_w_in, ffn1_w_out, ffn1_post_g, mix_pre_g, w_in, a_kv_norm_g, a_w_kv_up, c_w_alpha_up, c_b_alpha, c_norm_g, w_branch_a, w_branch_b, w_branch_c, w_gate, b_gate, w_out, mix_post_g, ffn2_pre_g, ffn2_w_in, ffn2_w_out, ffn2_post_g):
    bsz, seq, d = x.shape
    depth = w_in.shape[0]
    top_k = min(TOPK_MAX, seq // 4)
    tables = _rope_tables(positions)
    row = lambda a: a.reshape(1, -1)
    h = x.reshape(bsz * seq, d)
    for l in range(depth):
        h = _ffn(h, row(ffn1_pre_g[l]), ffn1_w_in[l].astype(BF16), ffn1_w_out[l].astype(BF16),
                 row(ffn1_post_g[l]))

        kv_up = a_w_kv_up[l].reshape(A_KV_RANK, A_HEADS, 2, HEAD_DIM)
        w_kup = kv_up[:, :, 0].reshape(A_KV_RANK, A_HEADS * HEAD_DIM).astype(BF16)
        w_vup_t = kv_up[:, :, 1].reshape(A_KV_RANK, A_HEADS // 2, LANES).transpose(1, 2, 0)
        w_vup_t = jnp.pad(w_vup_t, ((0, 0), (0, VT_ROWS - LANES), (0, 0)))
        w_vup_t = w_vup_t.reshape(A_HEADS // 2 * VT_ROWS, A_KV_RANK).astype(BF16)
        w_alpha = jnp.zeros((LANES, C_HEADS * C_HEAD_K), F32)
        w_alpha = w_alpha.at[SMALL_CA:SMALL_CA + C_ALPHA_RANK].set(c_w_alpha_up[l]).astype(BF16)
        p = _mixer_in(h, row(mix_pre_g[l]), _split_w_in(w_in[l]), *tables, row(a_kv_norm_g[l]),
                      w_kup, w_vup_t, w_alpha, row(c_b_alpha[l]), seq=seq)

        seq3 = lambda a: a.reshape(bsz, seq, a.shape[-1])
        v_t = p["v_t"].reshape(bsz, seq // DSA_KEY_CHUNK, -1, DSA_KEY_CHUNK)
        y_a = _dsa(seq3(p["q_i"]), seq3(p["small"]), seq3(p["q_a"]), seq3(p["k_i"]), seq3(p["k_a"]),
                   v_t, top_k=top_k).reshape(bsz * seq, -1)
        o_b, lse_b = [], []
        for g in range(len(B_GROUPS)):
            o_g, lse_g = _dilated_group(p[f"b_q{g}"], p[f"b_k{g}"], p[f"b_v{g}"])
            o_b.append(o_g)
            lse_b.append(lse_g)
        y_c = _gla(seq3(p["c_q"]), seq3(p["c_k"]), seq3(p["c_v"]), seq3(p["c_g"]), seq3(p["la"]),
                   row(c_norm_g[l])).reshape(bsz * seq, -1)

        h = _mixer_out(h, y_a, o_b, lse_b, y_c, row(mix_pre_g[l]), w_gate[l].astype(BF16),
                       row(b_gate[l]), w_branch_a[l].astype(BF16), w_branch_b[l].astype(BF16),
                       w_branch_c[l].astype(BF16), w_out[l].astype(BF16), row(mix_post_g[l]), seq=seq)

        h = _ffn(h, row(ffn2_pre_g[l]), ffn2_w_in[l].astype(BF16), ffn2_w_out[l].astype(BF16),
                 row(ffn2_post_g[l]))
    return h.reshape(bsz, seq, d)
```

```python
import functools

import jax
import jax.numpy as jnp
from jax import lax
from jax.experimental import pallas as pl
from jax.experimental.pallas import tpu as pltpu

F32 = jnp.float32
BF16 = jnp.bfloat16

HEAD_DIM = 64
ROT_DIM = HEAD_DIM // 4
ROPE_THETA = 500000.0
RMS_EPS = 1e-6
MACARON_WEIGHT = 0.5

A_HEADS = 8
A_KV_RANK = 128
IDX_HEADS = 8
IDX_DIM = 64
TOPK_MAX = 256

B_GROUPS = ((128, 1), (512, 4), (2048, 16))
B_HEADS_PER_GROUP = 4
B_HEADS = B_HEADS_PER_GROUP * len(B_GROUPS)

C_HEADS = 4
C_HEAD_K = 64
C_HEAD_V = 128
C_ALPHA_RANK = 16
C_GATE_TAU = 16.0
C_CHUNK = 64

LANES = 128
MXU_TILE = 256
Q_BLOCK = 128
DSA_Q_BLOCK = 256
DSA_KEY_CHUNK = 512
BF16_SUBLANES = 16
VT_ROWS = LANES + BF16_SUBLANES
BISECT_WARMUP = 14
BISECT_UNROLL = 2
BISECT_MAX_ITERS = 16
COUNT_SLABS = 16
MASKED_LOGIT = -1e30
LOG2_E = 1.4426950408889634
VMEM_LIMIT = 60 * 1024 * 1024

SMALL_IK = 0
SMALL_IW = IDX_DIM
SMALL_CA = IDX_DIM + IDX_HEADS


def _dot(a, b):
    return jnp.dot(a, b, preferred_element_type=F32)


def _dot_nt(a, b):
    return lax.dot_general(a, b, (((1,), (1,)), ((), ())), preferred_element_type=F32)


def _dot_tn(a, b):
    return lax.dot_general(a, b, (((0,), (0,)), ((), ())), preferred_element_type=F32)


def _rms(x, g):
    return x * lax.rsqrt(jnp.mean(x * x, axis=-1, keepdims=True) + RMS_EPS) * g


def _silu(x):
    return x * jax.nn.sigmoid(x)


def _params(n_grid):
    return pltpu.CompilerParams(dimension_semantics=("arbitrary",) * n_grid,
                                vmem_limit_bytes=VMEM_LIMIT)


def _resident(shape):
    zeros = (0,) * len(shape)
    return pl.BlockSpec(shape, lambda *_: zeros, pipeline_mode=pl.Buffered(1))


def _half_mask(shape, upper):
    lane = lax.broadcasted_iota(jnp.int32, shape, len(shape) - 1) % LANES
    return (lane >= HEAD_DIM) if upper else (lane < HEAD_DIM)


def _ffn_body(h_ref, pre_g_ref, w_in_ref, w_out_ref, post_g_ref, o_ref, *, d_ff, cuts):
    x = h_ref[...]
    xn = _rms(x, pre_g_ref[...]).astype(BF16)
    acc = jnp.zeros(x.shape, F32)
    for lo, hi in zip(cuts[:-1], cuts[1:]):
        gate = _dot(xn, w_in_ref[:, lo:hi])
        up = _dot(xn, w_in_ref[:, d_ff + lo:d_ff + hi])
        act = (_silu(gate) * up).astype(BF16)
        acc = acc + _dot(act, w_out_ref[lo:hi, :])
    o_ref[...] = x + MACARON_WEIGHT * _rms(acc, post_g_ref[...])


def _ffn(h, pre_g, w_in, w_out, post_g, *, tm=512):
    t, d = h.shape
    d_ff = w_out.shape[0]
    mid = (d_ff // 2 + MXU_TILE - 1) // MXU_TILE * MXU_TILE
    cuts = (0, mid, d_ff) if 0 < mid < d_ff else (0, d_ff)
    row = lambda i: (i, 0)
    return pl.pallas_call(
        functools.partial(_ffn_body, d_ff=d_ff, cuts=cuts),
        grid=(t // tm,),
        in_specs=[pl.BlockSpec((tm, d), row), _resident((1, d)), _resident(w_in.shape),
                  _resident(w_out.shape), _resident((1, d))],
        out_specs=pl.BlockSpec((tm, d), row),
        out_shape=jax.ShapeDtypeStruct((t, d), F32),
        compiler_params=_params(1),
        name="ffn",
    )(h, pre_g, w_in, w_out, post_g)


_IN_COLS = (("a_q", 512), ("i_q", 512), ("b_q", 768), ("b_k", 768), ("b_v", 768),
            ("c_q", 256), ("c_k", 256), ("c_v", 512), ("c_g", 512), ("ckv", 128), ("small", 128))


def _in_col_offsets():
    offs, o = {}, 0
    for name, n in _IN_COLS:
        offs[name] = (o, o + n)
        o += n
    return offs, o


def _rope(x, cos_t, sinp_t, sinm_t):
    outs = []
    for i in range(x.shape[1] // LANES):
        xs = x[:, i * LANES:(i + 1) * LANES]
        outs.append(xs * cos_t + pltpu.roll(xs, 8, 1) * sinp_t + pltpu.roll(xs, LANES - 8, 1) * sinm_t)
    return outs[0] if len(outs) == 1 else jnp.concatenate(outs, axis=1)


def _log_sigmoid(x):
    return jnp.minimum(x, 0.0) - jnp.log1p(jnp.exp(-jnp.abs(x)))


def _mixer_in_body(h_ref, g_ref, w_ref, cos_ref, sinp_ref, sinm_ref, kvg_ref, wk_ref, wv_ref,
                   wal_ref, bal_ref,
                   qa_ref, ka_ref, vt_ref, qi_ref, ki_ref, small_ref,
                   bq0_ref, bq1_ref, bq2_ref, bk0_ref, bk1_ref, bk2_ref, bv0_ref, bv1_ref, bv2_ref,
                   cq_ref, ck_ref, cv_ref, cg_ref, la_ref, stage_ref):
    offs, _ = _in_col_offsets()
    u = _rms(h_ref[...], g_ref[...]).astype(BF16)
    cos_t, sinp_t, sinm_t = cos_ref[...], sinp_ref[...], sinm_ref[...]
    tm = h_ref.shape[0]
    gw = B_HEADS_PER_GROUP * HEAD_DIM

    def proj(name):
        lo, hi = offs[name]
        return _dot(u, w_ref[:, lo:hi])

    def emit_dilated(x, out_refs):
        for j in range(x.shape[1] // LANES):
            stage_ref[j] = x[:, j * LANES:(j + 1) * LANES]
        per_g = gw // LANES
        for g, (_, dil) in enumerate(B_GROUPS):
            for r in range(dil):
                rows = [stage_ref[g * per_g + j, pl.ds(r, tm // dil, stride=dil), :] for j in range(per_g)]
                out_refs[g][0, r] = jnp.concatenate(rows, axis=1).astype(BF16)

    rope = lambda x: _rope(x, cos_t, sinp_t, sinm_t)
    q_scale = HEAD_DIM ** -0.5
    qa_ref[...] = (rope(proj("a_q")) * (q_scale * LOG2_E)).astype(BF16)
    qi_ref[...] = (rope(proj("i_q")) * (IDX_DIM ** -0.5)).astype(BF16)
    emit_dilated(rope(proj("b_q")) * q_scale, (bq0_ref, bq1_ref, bq2_ref))
    emit_dilated(rope(proj("b_k")), (bk0_ref, bk1_ref, bk2_ref))
    emit_dilated(proj("b_v"), (bv0_ref, bv1_ref, bv2_ref))
    cq_ref[...] = proj("c_q") * (C_HEAD_K ** -0.5)
    ck_ref[...] = proj("c_k")
    cv_ref[...] = proj("c_v")
    cg_ref[...] = proj("c_g")

    ckv = _rms(proj("ckv"), kvg_ref[...]).astype(BF16)
    ka_ref[...] = rope(_dot(ckv, wk_ref[...])).astype(BF16)
    v_t = _dot_nt(wv_ref[...], ckv)
    slab_row = lax.broadcasted_iota(jnp.int32, v_t.shape, 0) % VT_ROWS
    vt_ref[0] = jnp.where(slab_row >= LANES, 1.0, v_t).astype(BF16)

    small = proj("small")
    small_r = rope(small)
    ki_ref[...] = jnp.where(_half_mask(small.shape, False), small_r,
                            pltpu.roll(small_r, HEAD_DIM, 1)).astype(BF16)
    small_ref[...] = small
    la = _dot(small.astype(BF16), wal_ref[...]) + bal_ref[...]
    la_ref[...] = _log_sigmoid(la) * (1.0 / C_GATE_TAU)


def _mixer_in(h, g, w_big, cos_t, sinp_t, sinm_t, kv_g, w_kup, w_vup_t, w_alpha, b_alpha, *, seq):
    t, d = h.shape
    tm = DSA_KEY_CHUNK
    tiles = seq // tm
    gw = B_HEADS_PER_GROUP * HEAD_DIM
    row = lambda i: (i, 0)
    out_defs = [("q_a", 512, BF16), ("k_a", 512, BF16), ("v_t", "vt", BF16), ("q_i", 512, BF16),
                ("k_i", 128, BF16), ("small", LANES, F32)]
    out_defs += [(f"b_{nm}{gi}", ("dil", dil), BF16) for nm in "qkv" for gi, (_, dil) in enumerate(B_GROUPS)]
    out_defs += [("c_q", 256, F32), ("c_k", 256, F32), ("c_v", 512, F32), ("c_g", 512, F32),
                 ("la", 256, F32)]
    out_specs, out_shape = [], []
    for _, n, dt in out_defs:
        if n == "vt":
            out_specs.append(pl.BlockSpec((1, w_vup_t.shape[0], tm), lambda i: (i, 0, 0)))
            out_shape.append(jax.ShapeDtypeStruct((t // tm, w_vup_t.shape[0], tm), dt))
        elif isinstance(n, tuple):
            dil = n[1]
            out_specs.append(pl.BlockSpec((1, dil, tm // dil, gw),
                                          lambda i: (i // tiles, 0, i % tiles, 0)))
            out_shape.append(jax.ShapeDtypeStruct((t // seq, dil, seq // dil, gw), dt))
        else:
            out_specs.append(pl.BlockSpec((tm, n), row))
            out_shape.append(jax.ShapeDtypeStruct((t, n), dt))
    w_vup = w_vup_t
    outs = pl.pallas_call(
        _mixer_in_body,
        grid=(t // tm,),
        in_specs=[pl.BlockSpec((tm, d), row), _resident((1, d)), _resident(w_big.shape),
                  pl.BlockSpec((tm, LANES), row), pl.BlockSpec((tm, LANES), row),
                  pl.BlockSpec((tm, LANES), row), _resident(kv_g.shape), _resident(w_kup.shape),
                  _resident(w_vup.shape), _resident(w_alpha.shape), _resident(b_alpha.shape)],
        out_specs=out_specs,
        out_shape=out_shape,
        scratch_shapes=[pltpu.VMEM((B_HEADS * HEAD_DIM // LANES, tm, LANES), F32)],
        compiler_params=_params(1),
        name="mixer_in",
    )(h, g, w_big, cos_t, sinp_t, sinm_t, kv_g, w_kup, w_vup, w_alpha, b_alpha)
    return {name: o for (name, _, _), o in zip(out_defs, outs)}


def _masked_pair_rows(x, n_heads):
    out = []
    for h in range(n_heads):
        pair = x[:, (h // 2) * LANES:(h // 2 + 1) * LANES]
        out.append(jnp.where(_half_mask(pair.shape, h % 2 == 1), pair, jnp.zeros_like(pair)))
    return out


def _dsa_body(qi_ref, small_ref, qa_ref, ki_ref, ka_ref, vt_ref, o_ref,
              score_ref, acc_ref, qm_ref, s0_ref, s1_ref, *, seq, top_k):
    kc = DSA_KEY_CHUNK
    qb = DSA_Q_BLOCK
    i = pl.program_id(1)
    n_chunks = (i * qb + qb + kc - 1) // kc
    q_pos = i * qb + lax.broadcasted_iota(jnp.int32, (1, qb), 1)
    key_off = lax.broadcasted_iota(jnp.int32, (kc, qb), 0)

    n_pairs = A_HEADS // 2
    qi_heads = _masked_pair_rows(qi_ref[0], IDX_HEADS)
    for pr in range(n_pairs):
        qm_ref[pr] = jnp.concatenate([qi_heads[2 * pr], qi_heads[2 * pr + 1]], axis=0)
    w_t = jnp.transpose(small_ref[0])[SMALL_IW:SMALL_IW + IDX_HEADS, :] * (IDX_HEADS ** -0.5)

    def skewed(first, stage, carry):
        def step(j, st):
            carry, aux0 = st
            aux1 = first(2 * j + 1, s1_ref)
            carry = stage(2 * j, s0_ref, carry, aux0)
            aux0 = first(2 * j + 2, s0_ref)
            return stage(2 * j + 1, s1_ref, carry, aux1), aux0
        return lax.fori_loop(0, (n_chunks + 1) // 2, step, (carry, first(0, s0_ref)))[0]

    def idx_logits_stage(c, s_ref):
        off = pl.multiple_of(jnp.minimum(c, n_chunks - 1) * kc, kc)
        k_c = ki_ref[0, pl.ds(off, kc), :]
        for pr in range(n_pairs):
            s_ref[pr] = _dot_nt(k_c, qm_ref[pr])
        return 0

    def score_stage(c, s_ref, carry, _):
        mx, mn = carry
        cc = jnp.minimum(c, n_chunks - 1)
        sc = jnp.zeros((kc, qb), F32)
        for h in range(IDX_HEADS):
            logit = s_ref[h // 2, :, (h % 2) * qb:(h % 2 + 1) * qb]
            sc = sc + jnp.maximum(logit, 0.0) * w_t[h:h + 1, :]
        causal = (cc * kc + key_off) <= q_pos
        score_ref[cc] = jnp.where(causal, sc, -jnp.inf)
        mx = jnp.maximum(mx, jnp.max(jnp.where(causal, sc, -jnp.inf), axis=0, keepdims=True))
        mn = jnp.minimum(mn, jnp.min(jnp.where(causal, sc, jnp.inf), axis=0, keepdims=True))
        return mx, mn

    row_max, row_min = skewed(
        idx_logits_stage, score_stage,
        (jnp.full((1, qb), -jnp.inf, F32), jnp.full((1, qb), jnp.inf, F32)))

    slab = kc // COUNT_SLABS

    def sweep(init, fold, per_slab):
        def step(c, acc):
            for j in range(COUNT_SLABS):
                acc = fold(acc, per_slab(score_ref[c, j * slab:(j + 1) * slab, :]))
            return acc
        return lax.fori_loop(0, n_chunks, step, jnp.full((slab, qb), init, F32))

    def count_ge(thr):
        acc = sweep(0.0, lambda a, v: a + v, lambda x: jnp.where(x >= thr, 1.0, 0.0))
        return jnp.sum(acc, axis=0, keepdims=True)

    def max_below(thr):
        acc = sweep(-jnp.inf, jnp.maximum, lambda x: jnp.where(x < thr, x, -jnp.inf))
        return jnp.max(acc, axis=0, keepdims=True)

    k_f = float(top_k)
    n_causal = (q_pos + 1).astype(F32)
    few = n_causal <= k_f
    lo0 = row_min
    hi0 = row_max + jnp.abs(row_max) + 1.0

    def halve(st):
        lo, hi, c_lo, c_hi = st
        mid = lo + (hi - lo) * 0.5
        cnt = count_ge(mid)
        ge = cnt >= k_f
        return (jnp.where(ge, mid, lo), jnp.where(ge, hi, mid),
                jnp.where(ge, cnt, c_lo), jnp.where(ge, c_hi, cnt))

    def snap(st):
        lo, hi, c_lo, c_hi = st
        v = max_below(hi)
        cnt = count_ge(v)
        ge = jnp.logical_and(cnt >= k_f, jnp.logical_not(few))
        lower = jnp.logical_and(cnt < k_f, jnp.logical_not(few))
        return (jnp.where(ge, v, lo), jnp.where(lower, v, hi),
                jnp.where(ge, cnt, c_lo), jnp.where(lower, cnt, c_hi)), ge

    st0 = (lo0, hi0, n_causal, jnp.zeros((1, qb), F32))
    st1 = lax.fori_loop(0, BISECT_WARMUP, lambda _, st: halve(st), st0)

    def search_cond(carry):
        return jnp.logical_and(carry[0] < BISECT_MAX_ITERS, carry[2] > 0.5)

    def search_step(carry):
        it, st, _ = carry
        for _ in range(BISECT_UNROLL):
            st = halve(st)
        st, snapped = snap(st)
        done = few | (st[2] == k_f) | snapped
        return it + 1, st, jnp.max(jnp.where(done, 0.0, 1.0))

    pending0 = jnp.max(jnp.where(few | (st1[2] == k_f), 0.0, 1.0))
    _, (lo, hi, c_lo, c_hi), _ = lax.while_loop(search_cond, search_step, (jnp.int32(0), st1, pending0))

    tied = jnp.logical_and(jnp.logical_not(few), c_lo > k_f)
    need = k_f - c_hi

    def tie_cut(_):
        r_i = lax.broadcasted_iota(jnp.int32, (kc, kc), 0)
        c_i = lax.broadcasted_iota(jnp.int32, (kc, kc), 1)
        tri = jnp.where(r_i >= c_i, 1.0, 0.0).astype(BF16)

        def step(c, st):
            seen, j_best = st
            x = score_ref[c]
            tie = (x >= lo) & (x < hi)
            prefix = seen + _dot(tri, jnp.where(tie, 1.0, 0.0).astype(BF16))
            idx = (c * kc + key_off).astype(F32)
            kept = jnp.where(tie & (prefix <= need), idx, -1.0)
            return prefix[kc - 1:kc, :], jnp.maximum(j_best, jnp.max(kept, axis=0, keepdims=True))
        return lax.fori_loop(0, n_chunks, step,
                             (jnp.zeros((1, qb), F32), jnp.full((1, qb), -1.0, F32)))[1]

    any_tied = jnp.max(jnp.where(tied, 1.0, 0.0)) > 0.5
    j_cut = lax.cond(any_tied, tie_cut, lambda _: jnp.full((1, qb), float(seq), F32), 0)
    j_cut = jnp.where(tied, j_cut, float(seq))
    thr_hi = jnp.where(tied, hi, lo)

    qa_heads = _masked_pair_rows(qa_ref[0], A_HEADS)
    for pr in range(n_pairs):
        qm_ref[pr] = jnp.concatenate([qa_heads[2 * pr], qa_heads[2 * pr + 1]], axis=0)
    acc_ref[...] = jnp.zeros(acc_ref.shape, F32)

    def attn_logits_stage(c, s_ref):
        cc = jnp.minimum(c, n_chunks - 1)
        off = pl.multiple_of(cc * kc, kc)
        x = score_ref[cc]
        sel = (x >= thr_hi) | ((x >= lo) & ((cc * kc + key_off).astype(F32) <= j_cut))
        bias = jnp.where(sel & (c < n_chunks), 0.0, MASKED_LOGIT)
        col_max = []
        for pr in range(n_pairs):
            s2 = _dot_nt(ka_ref[0, pl.ds(off, kc), pr * LANES:(pr + 1) * LANES], qm_ref[pr])
            for e in range(2):
                s = s2[:, e * qb:(e + 1) * qb] + bias
                s_ref[pr, :, e * qb:(e + 1) * qb] = s
                col_max.append(jnp.max(s, axis=0, keepdims=True))
        return tuple(col_max)

    def softmax_pv_stage(c, s_ref, m_all, col_max):
        cc = jnp.minimum(c, n_chunks - 1)
        m_out = []
        for pr in range(n_pairs):
            p_halves, alphas = [], []
            for e in range(2):
                h = 2 * pr + e
                m_new = jnp.maximum(m_all[h], col_max[h])
                alphas.append(jnp.exp2(m_all[h] - m_new))
                p_halves.append(jnp.exp2(s_ref[pr, :, e * qb:(e + 1) * qb] - m_new).astype(BF16))
                m_out.append(m_new)
            v_c = vt_ref[0, cc, pr * VT_ROWS:(pr + 1) * VT_ROWS, :]
            pv = _dot(v_c, jnp.concatenate(p_halves, axis=1))
            acc_ref[pr] = acc_ref[pr] * jnp.concatenate(alphas, axis=1) + pv
        return tuple(m_out)

    m0 = tuple(jnp.full((1, qb), MASKED_LOGIT, F32) for _ in range(A_HEADS))
    skewed(attn_logits_stage, softmax_pv_stage, m0)

    for pr in range(n_pairs):
        a = acc_ref[pr]
        denom = a[LANES:LANES + 1, :]
        o_t = jnp.concatenate([a[:HEAD_DIM, :qb] / denom[:, :qb],
                               a[HEAD_DIM:LANES, qb:] / denom[:, qb:]], axis=0)
        o_ref[0, :, pr * LANES:(pr + 1) * LANES] = jnp.transpose(o_t).astype(BF16)


def _dsa(q_i, small, q_a, k_i, k_a, v_t, *, top_k):
    bsz, seq, _ = q_a.shape
    qb, kc = DSA_Q_BLOCK, DSA_KEY_CHUNK
    blk = lambda b, i: (b, i, 0)
    whole = lambda shape: pl.BlockSpec((1,) + shape, lambda b, i: (b,) + (0,) * len(shape),
                                       pipeline_mode=pl.Buffered(1))
    return pl.pallas_call(
        functools.partial(_dsa_body, seq=seq, top_k=top_k),
        grid=(bsz, seq // qb),
        in_specs=[pl.BlockSpec((1, qb, 512), blk), pl.BlockSpec((1, qb, LANES), blk),
                  pl.BlockSpec((1, qb, 512), blk), whole((seq, LANES)), whole((seq, 512)),
                  whole(v_t.shape[1:])],
        out_specs=pl.BlockSpec((1, qb, 512), blk),
        out_shape=jax.ShapeDtypeStruct((bsz, seq, 512), BF16),
        scratch_shapes=[pltpu.VMEM((seq // kc, kc, qb), F32),
                        pltpu.VMEM((A_HEADS // 2, VT_ROWS, 2 * qb), F32),
                        pltpu.VMEM((A_HEADS // 2, 2 * qb, LANES), BF16),
                        pltpu.VMEM((A_HEADS // 2, kc, 2 * qb), F32),
                        pltpu.VMEM((A_HEADS // 2, kc, 2 * qb), F32)],
        compiler_params=_params(2),
        name="dsa",
    )(q_i, small, q_a, k_i, k_a, v_t)


def _dilated_body(q_ref, kp_ref, k_ref, vp_ref, v_ref, o_ref, lse_ref, *, tq):
    qb = Q_BLOCK
    n = pl.program_id(2)
    row = lax.broadcasted_iota(jnp.int32, (qb, 2 * qb), 0)
    col = lax.broadcasted_iota(jnp.int32, (qb, 2 * qb), 1)
    rel = row + qb - col
    band = (rel >= 0) & (rel <= qb)
    n_pairs = B_HEADS_PER_GROUP // 2
    for s in range(tq // qb):
        q = q_ref[0, 0, s * qb:(s + 1) * qb, :]
        if s == 0:
            k_prev, v_prev = kp_ref[0, 0], vp_ref[0, 0]
            valid = band & ((col >= qb) | (n > 0))
        else:
            k_prev = k_ref[0, 0, (s - 1) * qb:s * qb, :]
            v_prev = v_ref[0, 0, (s - 1) * qb:s * qb, :]
            valid = band
        kk = jnp.concatenate([k_prev, k_ref[0, 0, s * qb:(s + 1) * qb, :]], axis=0)
        vv = jnp.concatenate([v_prev, v_ref[0, 0, s * qb:(s + 1) * qb, :]], axis=0)
        o_parts, lse_parts = [], []
        for pr in range(n_pairs):
            q_pair = q[:, pr * LANES:(pr + 1) * LANES]
            k_pair = kk[:, pr * LANES:(pr + 1) * LANES]
            v_pair = vv[:, pr * LANES:(pr + 1) * LANES]
            o_h, lse_h = [], []
            for odd in (False, True):
                q_h = jnp.where(_half_mask(q_pair.shape, odd), q_pair, jnp.zeros_like(q_pair))
                sc = jnp.where(valid, _dot_nt(q_h, k_pair), -jnp.inf)
                m = jnp.max(sc, axis=1, keepdims=True)
                p = jnp.exp(sc - m)
                l = jnp.sum(p, axis=1, keepdims=True)
                o_h.append(_dot(p.astype(BF16), v_pair) / l)
                lse_h.append(jnp.broadcast_to(m + jnp.log(l), (qb, LANES)))
            lower = _half_mask((qb, LANES), False)
            o_parts.append(jnp.where(lower, o_h[0], o_h[1]))
            lse_parts.append(jnp.where(lower, lse_h[0], lse_h[1]))
        o_ref[0, 0, s * qb:(s + 1) * qb, :] = jnp.concatenate(o_parts, axis=1)
        lse_ref[0, 0, s * qb:(s + 1) * qb, :] = jnp.concatenate(lse_parts, axis=1)


def _dilated_group(b_q, b_k, b_v):
    bsz, dil, sub, gw = b_q.shape
    qb = Q_BLOCK
    tq = min(512, sub)
    cur = lambda b, r, n: (b, r, n, 0)
    prev = lambda b, r, n: (b, r, jnp.maximum(n * (tq // qb) - 1, 0), 0)
    return pl.pallas_call(
        functools.partial(_dilated_body, tq=tq),
        grid=(bsz, dil, sub // tq),
        in_specs=[pl.BlockSpec((1, 1, tq, gw), cur),
                  pl.BlockSpec((1, 1, qb, gw), prev), pl.BlockSpec((1, 1, tq, gw), cur),
                  pl.BlockSpec((1, 1, qb, gw), prev), pl.BlockSpec((1, 1, tq, gw), cur)],
        out_specs=[pl.BlockSpec((1, 1, tq, gw), cur), pl.BlockSpec((1, 1, tq, gw), cur)],
        out_shape=[jax.ShapeDtypeStruct(b_q.shape, F32)] * 2,
        compiler_params=_params(3),
        name=f"dilated_d{dil}",
    )(b_q, b_k, b_k, b_v, b_v)


def _gla_body(q_ref, k_ref, v_ref, g_ref, la_ref, ng_ref, o_ref, st_ref, *, n_sub):
    cs = C_CHUNK
    bsz = q_ref.shape[0]

    @pl.when(pl.program_id(0) == 0)
    def _():
        st_ref[...] = jnp.zeros(st_ref.shape, F32)

    r_i = lax.broadcasted_iota(jnp.int32, (cs, cs), 0)
    c_i = lax.broadcasted_iota(jnp.int32, (cs, cs), 1)
    tril = r_i >= c_i
    tri_ones = jnp.where(tril, 1.0, 0.0).astype(BF16)
    ng = ng_ref[...]
    for c in range(n_sub):
        rows = slice(c * cs, (c + 1) * cs)
        for bi in range(bsz):
            la = la_ref[bi, rows, :]
            la_hi = la.astype(BF16)
            la_lo = (la - la_hi.astype(F32)).astype(BF16)
            b = _dot(tri_ones, la_hi) + _dot(tri_ones, la_lo)
            b_last = b[cs - 1:cs, :]
            q = q_ref[bi, rows, :]
            k = k_ref[bi, rows, :]
            q_dec = q * jnp.exp(b)
            k_dec = (k * jnp.exp(-b)).astype(BF16)
            k_end = (k * jnp.exp(b_last - b)).astype(BF16)
            decay = jnp.exp(b_last)
            o_heads = []
            for h in range(C_HEADS):
                p_lo = (h // 2) * LANES
                q_pair = q_dec[:, p_lo:p_lo + LANES]
                q_h = jnp.where(_half_mask(q_pair.shape, h % 2 == 1), q_pair, 0.0).astype(BF16)
                v_h = v_ref[bi, rows, h * C_HEAD_V:(h + 1) * C_HEAD_V].astype(BF16)
                attn = jnp.where(tril, _dot_nt(q_h, k_dec[:, p_lo:p_lo + LANES]), 0.0)
                st = st_ref[bi * C_HEADS + h]
                o_h = _dot(attn.astype(BF16), v_h) + _dot_nt(q_h, st.astype(BF16))
                st_ref[bi * C_HEADS + h] = (st * decay[:, p_lo:p_lo + LANES]
                                            + _dot_tn(v_h, k_end[:, p_lo:p_lo + LANES]))
                gate = g_ref[bi, rows, h * C_HEAD_V:(h + 1) * C_HEAD_V]
                o_heads.append(_rms(o_h, ng) * _silu(gate))
            o_ref[bi, rows, :] = jnp.concatenate(o_heads, axis=1).astype(BF16)


def _gla(c_q, c_k, c_v, c_g, la, norm_g, *, n_sub=8):
    bsz, seq, _ = c_q.shape
    rows = n_sub * C_CHUNK
    blk = lambda n: (0, n, 0)
    kw, vw = C_HEADS * C_HEAD_K, C_HEADS * C_HEAD_V
    return pl.pallas_call(
        functools.partial(_gla_body, n_sub=n_sub),
        grid=(seq // rows,),
        in_specs=[pl.BlockSpec((bsz, rows, kw), blk), pl.BlockSpec((bsz, rows, kw), blk),
                  pl.BlockSpec((bsz, rows, vw), blk), pl.BlockSpec((bsz, rows, vw), blk),
                  pl.BlockSpec((bsz, rows, kw), blk), _resident(norm_g.shape)],
        out_specs=pl.BlockSpec((bsz, rows, vw), blk),
        out_shape=jax.ShapeDtypeStruct((bsz, seq, vw), BF16),
        scratch_shapes=[pltpu.VMEM((bsz * C_HEADS, C_HEAD_V, LANES), F32)],
        compiler_params=_params(1),
        name="gla",
    )(c_q, c_k, c_v, c_g, la, norm_g)


def _mixer_out_body(h_ref, ya_ref, ob0_ref, ob1_ref, ob2_ref, ls0_ref, ls1_ref, ls2_ref, yc_ref,
                    pre_g_ref, wg_ref, bg_ref, wa_ref, wb_ref, wc_ref, wo_ref, post_g_ref, o_ref,
                    stage_ref):
    x = h_ref[...]
    tm, d = x.shape
    u = _rms(x, pre_g_ref[...]).astype(BF16)

    def token_order(blk_ref, slot):
        dil = blk_ref.shape[1]
        if dil == 1:
            return blk_ref[0, 0]
        n_l = blk_ref.shape[3] // LANES
        for r in range(dil):
            for j in range(n_l):
                stage_ref[slot * n_l + j, pl.ds(r, tm // dil, stride=dil), :] = (
                    blk_ref[0, r, :, j * LANES:(j + 1) * LANES])
        return jnp.concatenate([stage_ref[slot * n_l + j] for j in range(n_l)], axis=1)

    ls = [token_order(r, j) for j, r in enumerate((ls0_ref, ls1_ref, ls2_ref))]
    ob = [token_order(r, 3 + j) for j, r in enumerate((ob0_ref, ob1_ref, ob2_ref))]
    mx = jnp.maximum(jnp.maximum(ls[0], ls[1]), ls[2])
    ew = [jnp.exp(v - mx) for v in ls]
    y_b = ((ew[0] * ob[0] + ew[1] * ob[1] + ew[2] * ob[2]) / (ew[0] + ew[1] + ew[2])).astype(BF16)

    branches = ((ya_ref[...], wa_ref), (y_b, wb_ref), (yc_ref[...], wc_ref))
    merged = jnp.zeros(x.shape, F32)
    for i, (y, w_ref) in enumerate(branches):
        gate = jax.nn.sigmoid(_dot(u, wg_ref[:, i * d:(i + 1) * d]) + bg_ref[:, i * d:(i + 1) * d])
        merged = merged + gate * _dot(y, w_ref[...])
    m = _dot(merged.astype(BF16), wo_ref[...])
    o_ref[...] = x + _rms(m, post_g_ref[...])


def _mixer_out(h, y_a, o_b, lse_b, y_c, pre_g, w_gate, b_gate, w_a, w_b, w_c, w_o, post_g, *, seq,
               tm=512):
    t, d = h.shape
    tiles = seq // tm
    row = lambda i: (i, 0)

    def tile(a):
        if a.ndim == 2:
            return pl.BlockSpec((tm, a.shape[1]), row)
        dil, gw = a.shape[1], a.shape[3]
        return pl.BlockSpec((1, dil, tm // dil, gw), lambda i: (i // tiles, 0, i % tiles, 0))

    acts = [h, y_a, *o_b, *lse_b, y_c]
    consts = [pre_g, w_gate, b_gate, w_a, w_b, w_c, w_o, post_g]
    return pl.pallas_call(
        _mixer_out_body,
        grid=(t // tm,),
        in_specs=[tile(a) for a in acts] + [_resident(c.shape) for c in consts],
        out_specs=pl.BlockSpec((tm, d), row),
        out_shape=jax.ShapeDtypeStruct((t, d), F32),
        scratch_shapes=[pltpu.VMEM((2 * len(o_b) * o_b[0].shape[3] // LANES, tm, LANES), F32)],
        compiler_params=_params(1),
        name="mixer_out",
    )(*acts, *consts)


def _rope_tables(positions):
    half = ROT_DIM // 2
    inv_freq = ROPE_THETA ** (-jnp.arange(0, ROT_DIM, 2, dtype=F32) / ROT_DIM)
    ang = positions.astype(F32).reshape(-1, 1) * inv_freq
    cos, sin = jnp.cos(ang), jnp.sin(ang)
    t = cos.shape[0]
    pad = lambda n: jnp.zeros((t, n), F32)
    cos_t = jnp.concatenate([cos, cos, jnp.ones((t, HEAD_DIM - 2 * half), F32)], axis=1)
    sinp_t = jnp.concatenate([pad(half), sin, pad(HEAD_DIM - 2 * half)], axis=1)
    sinm_t = jnp.concatenate([-sin, pad(HEAD_DIM - half)], axis=1)
    rep = LANES // HEAD_DIM
    return tuple(jnp.tile(a, (1, rep)) for a in (cos_t, sinp_t, sinm_t))


def _split_w_in(w_in):
    d = w_in.shape[0]
    sizes = (A_HEADS * HEAD_DIM, A_KV_RANK, IDX_HEADS * IDX_DIM, IDX_DIM, IDX_HEADS,
             3 * B_HEADS * HEAD_DIM, C_HEADS * C_HEAD_K, C_HEADS * C_HEAD_K,
             C_HEADS * C_HEAD_V, C_HEADS * C_HEAD_V, C_ALPHA_RANK)
    names = ("a_q", "ckv", "i_q", "i_k", "i_w", "b_qkv", "c_q", "c_k", "c_v", "c_g", "c_a")
    parts, o = {}, 0
    for name, n in zip(names, sizes):
        parts[name] = w_in[:, o:o + n]
        o += n
    bw = B_HEADS * HEAD_DIM
    parts["b_q"], parts["b_k"], parts["b_v"] = (parts["b_qkv"][:, j * bw:(j + 1) * bw] for j in range(3))
    used = IDX_DIM + IDX_HEADS + C_ALPHA_RANK
    parts["small"] = jnp.concatenate(
        [parts["i_k"], parts["i_w"], parts["c_a"], jnp.zeros((d, LANES - used), w_in.dtype)], axis=1)
    return jnp.concatenate([parts[name] for name, _ in _IN_COLS], axis=1).astype(BF16)


def kernel(x, positions, ffn1_pre_g, ffn1_w_in, ffn1_w_out, ffn1_post_g, mix_pre_g, w_in, a_kv_norm_g, a_w_kv_up, c_w_alpha_up, c_b_alpha, c_norm_g, w_branch_a, w_branch_b, w_branch_c, w_gate, b_gate, w_out, mix_post_g, ffn2_pre_g, ffn2_w_in, ffn2_w_out, ffn2_post_g):
    bsz, seq, d = x.shape
    depth = w_in.shape[0]
    top_k = min(TOPK_MAX, seq // 4)
    tables = _rope_tables(positions)
    row = lambda a: a.reshape(1, -1)
    h = x.reshape(bsz * seq, d)
    for l in range(depth):
        h = _ffn(h, row(ffn1_pre_g[l]), ffn1_w_in[l].astype(BF16), ffn1_w_out[l].astype(BF16),
                 row(ffn1_post_g[l]))

        kv_up = a_w_kv_up[l].reshape(A_KV_RANK, A_HEADS, 2, HEAD_DIM)
        w_kup = kv_up[:, :, 0].reshape(A_KV_RANK, A_HEADS * HEAD_DIM).astype(BF16)
        w_vup_t = kv_up[:, :, 1].reshape(A_KV_RANK, A_HEADS // 2, LANES).transpose(1, 2, 0)
        w_vup_t = jnp.pad(w_vup_t, ((0, 0), (0, VT_ROWS - LANES), (0, 0)))
        w_vup_t = w_vup_t.reshape(A_HEADS // 2 * VT_ROWS, A_KV_RANK).astype(BF16)
        w_alpha = jnp.zeros((LANES, C_HEADS * C_HEAD_K), F32)
        w_alpha = w_alpha.at[SMALL_CA:SMALL_CA + C_ALPHA_RANK].set(c_w_alpha_up[l]).astype(BF16)
        p = _mixer_in(h, row(mix_pre_g[l]), _split_w_in(w_in[l]), *tables, row(a_kv_norm_g[l]),
                      w_kup, w_vup_t, w_alpha, row(c_b_alpha[l]), seq=seq)

        seq3 = lambda a: a.reshape(bsz, seq, a.shape[-1])
        v_t = p["v_t"].reshape(bsz, seq // DSA_KEY_CHUNK, -1, DSA_KEY_CHUNK)
        y_a = _dsa(seq3(p["q_i"]), seq3(p["small"]), seq3(p["q_a"]), seq3(p["k_i"]), seq3(p["k_a"]),
                   v_t, top_k=top_k).reshape(bsz * seq, -1)
        o_b, lse_b = [], []
        for g in range(len(B_GROUPS)):
            o_g, lse_g = _dilated_group(p[f"b_q{g}"], p[f"b_k{g}"], p[f"b_v{g}"])
            o_b.append(o_g)
            lse_b.append(lse_g)
        y_c = _gla(seq3(p["c_q"]), seq3(p["c_k"]), seq3(p["c_v"]), seq3(p["c_g"]), seq3(p["la"]),
                   row(c_norm_g[l])).reshape(bsz * seq, -1)

        h = _mixer_out(h, y_a, o_b, lse_b, y_c, row(mix_pre_g[l]), w_gate[l].astype(BF16),
                       row(b_gate[l]), w_branch_a[l].astype(BF16), w_branch_b[l].astype(BF16),
                       w_branch_c[l].astype(BF16), w_out[l].astype(BF16), row(mix_post_g[l]), seq=seq)

        h = _ffn(h, row(ffn2_pre_g[l]), ffn2_w_in[l].astype(BF16), ffn2_w_out[l].astype(BF16),
                 row(ffn2_post_g[l]))
    return h.reshape(bsz, seq, d)
```

```python
import functools

import jax
import jax.numpy as jnp
from jax import lax
from jax.experimental import pallas as pl
from jax.experimental.pallas import tpu as pltpu

F32 = jnp.float32
BF16 = jnp.bfloat16

HEAD_DIM = 64
ROT_DIM = HEAD_DIM // 4
ROPE_THETA = 500000.0
RMS_EPS = 1e-6
MACARON_WEIGHT = 0.5

A_HEADS = 8
A_KV_RANK = 128
IDX_HEADS = 8
IDX_DIM = 64
TOPK_MAX = 256

B_GROUPS = ((128, 1), (512, 4), (2048, 16))
B_HEADS_PER_GROUP = 4
B_HEADS = B_HEADS_PER_GROUP * len(B_GROUPS)

C_HEADS = 4
C_HEAD_K = 64
C_HEAD_V = 128
C_ALPHA_RANK = 16
C_GATE_TAU = 16.0
C_CHUNK = 64

LANES = 128
MXU_TILE = 256
Q_BLOCK = 128
DSA_Q_BLOCK = 256
DSA_KEY_CHUNK = 512
BF16_SUBLANES = 16
VT_ROWS = LANES + BF16_SUBLANES
BISECT_WARMUP = 14
HI_MARGIN = 2.0 ** -10
BISECT_UNROLL = 2
BISECT_MAX_ITERS = 16
COUNT_SLABS = 16
MASKED_LOGIT = -1e30
LOG2_E = 1.4426950408889634
VMEM_LIMIT = 60 * 1024 * 1024

SMALL_IK = 0
SMALL_IW = IDX_DIM
SMALL_CA = IDX_DIM + IDX_HEADS


def _dot(a, b):
    return jnp.dot(a, b, preferred_element_type=F32)


def _dot_nt(a, b):
    return lax.dot_general(a, b, (((1,), (1,)), ((), ())), preferred_element_type=F32)


def _dot_tn(a, b):
    return lax.dot_general(a, b, (((0,), (0,)), ((), ())), preferred_element_type=F32)


def _rms(x, g):
    return x * lax.rsqrt(jnp.mean(x * x, axis=-1, keepdims=True) + RMS_EPS) * g


def _silu(x):
    return x * jax.nn.sigmoid(x)


def _params(n_grid):
    return pltpu.CompilerParams(dimension_semantics=("arbitrary",) * n_grid,
                                vmem_limit_bytes=VMEM_LIMIT)


def _resident(shape):
    zeros = (0,) * len(shape)
    return pl.BlockSpec(shape, lambda *_: zeros, pipeline_mode=pl.Buffered(1))


def _half_mask(shape, upper):
    lane = lax.broadcasted_iota(jnp.int32, shape, len(shape) - 1) % LANES
    return (lane >= HEAD_DIM) if upper else (lane < HEAD_DIM)


def _ffn_body(h_ref, pre_g_ref, w_in_ref, w_out_ref, post_g_ref, o_ref, *, d_ff, cuts):
    x = h_ref[...]
    xn = _rms(x, pre_g_ref[...]).astype(BF16)
    acc = jnp.zeros(x.shape, F32)
    for lo, hi in zip(cuts[:-1], cuts[1:]):
        gate = _dot(xn, w_in_ref[:, lo:hi])
        up = _dot(xn, w_in_ref[:, d_ff + lo:d_ff + hi])
        act = (_silu(gate) * up).astype(BF16)
        acc = acc + _dot(act, w_out_ref[lo:hi, :])
    o_ref[...] = x + MACARON_WEIGHT * _rms(acc, post_g_ref[...])


def _ffn(h, pre_g, w_in, w_out, post_g, *, tm=512):
    t, d = h.shape
    d_ff = w_out.shape[0]
    mid = (d_ff // 2 + MXU_TILE - 1) // MXU_TILE * MXU_TILE
    cuts = (0, mid, d_ff) if 0 < mid < d_ff else (0, d_ff)
    row = lambda i: (i, 0)
    return pl.pallas_call(
        functools.partial(_ffn_body, d_ff=d_ff, cuts=cuts),
        grid=(t // tm,),
        in_specs=[pl.BlockSpec((tm, d), row), _resident((1, d)), _resident(w_in.shape),
                  _resident(w_out.shape), _resident((1, d))],
        out_specs=pl.BlockSpec((tm, d), row),
        out_shape=jax.ShapeDtypeStruct((t, d), F32),
        compiler_params=_params(1),
        name="ffn",
    )(h, pre_g, w_in, w_out, post_g)


_IN_COLS = (("a_q", 512), ("i_q", 512), ("b_q", 768), ("b_k", 768), ("b_v", 768),
            ("c_q", 256), ("c_k", 256), ("c_v", 512), ("c_g", 512), ("ckv", 128), ("small", 128))


def _in_col_offsets():
    offs, o = {}, 0
    for name, n in _IN_COLS:
        offs[name] = (o, o + n)
        o += n
    return offs, o


def _rope(x, cos_t, sinp_t, sinm_t):
    outs = []
    for i in range(x.shape[1] // LANES):
        xs = x[:, i * LANES:(i + 1) * LANES]
        outs.append(xs * cos_t + pltpu.roll(xs, 8, 1) * sinp_t + pltpu.roll(xs, LANES - 8, 1) * sinm_t)
    return outs[0] if len(outs) == 1 else jnp.concatenate(outs, axis=1)


def _log_sigmoid(x):
    return jnp.minimum(x, 0.0) - jnp.log1p(jnp.exp(-jnp.abs(x)))


def _mixer_in_body(h_ref, g_ref, w_ref, cos_ref, sinp_ref, sinm_ref, kvg_ref, wk_ref, wv_ref,
                   wal_ref, bal_ref,
                   qa_ref, ka_ref, vt_ref, qi_ref, ki_ref, small_ref,
                   bq0_ref, bq1_ref, bq2_ref, bk0_ref, bk1_ref, bk2_ref, bv0_ref, bv1_ref, bv2_ref,
                   cq_ref, ck_ref, cv_ref, cg_ref, la_ref, stage_ref):
    offs, _ = _in_col_offsets()
    u = _rms(h_ref[...], g_ref[...]).astype(BF16)
    cos_t, sinp_t, sinm_t = cos_ref[...], sinp_ref[...], sinm_ref[...]
    tm = h_ref.shape[0]
    gw = B_HEADS_PER_GROUP * HEAD_DIM

    def proj(name):
        lo, hi = offs[name]
        return _dot(u, w_ref[:, lo:hi])

    def emit_dilated(x, out_refs):
        for j in range(x.shape[1] // LANES):
            stage_ref[j] = x[:, j * LANES:(j + 1) * LANES]
        per_g = gw // LANES
        for g, (_, dil) in enumerate(B_GROUPS):
            for r in range(dil):
                rows = [stage_ref[g * per_g + j, pl.ds(r, tm // dil, stride=dil), :] for j in range(per_g)]
                out_refs[g][0, r] = jnp.concatenate(rows, axis=1).astype(BF16)

    rope = lambda x: _rope(x, cos_t, sinp_t, sinm_t)
    q_scale = HEAD_DIM ** -0.5
    qa_ref[...] = (rope(proj("a_q")) * (q_scale * LOG2_E)).astype(BF16)
    qi_ref[...] = (rope(proj("i_q")) * (IDX_DIM ** -0.5)).astype(BF16)
    emit_dilated(rope(proj("b_q")) * q_scale, (bq0_ref, bq1_ref, bq2_ref))
    emit_dilated(rope(proj("b_k")), (bk0_ref, bk1_ref, bk2_ref))
    emit_dilated(proj("b_v"), (bv0_ref, bv1_ref, bv2_ref))
    cq_ref[...] = proj("c_q") * (C_HEAD_K ** -0.5)
    ck_ref[...] = proj("c_k")
    cv_ref[...] = proj("c_v")
    cg_ref[...] = proj("c_g")

    ckv = _rms(proj("ckv"), kvg_ref[...]).astype(BF16)
    ka_ref[...] = rope(_dot(ckv, wk_ref[...])).astype(BF16)
    v_t = _dot_nt(wv_ref[...], ckv)
    slab_row = lax.broadcasted_iota(jnp.int32, v_t.shape, 0) % VT_ROWS
    vt_ref[0] = jnp.where(slab_row >= LANES, 1.0, v_t).astype(BF16)

    small = proj("small")
    small_r = rope(small)
    ki_ref[...] = jnp.where(_half_mask(small.shape, False), small_r,
                            pltpu.roll(small_r, HEAD_DIM, 1)).astype(BF16)
    small_ref[...] = small
    la = _dot(small.astype(BF16), wal_ref[...]) + bal_ref[...]
    la_ref[...] = _log_sigmoid(la) * (1.0 / C_GATE_TAU)


def _mixer_in(h, g, w_big, cos_t, sinp_t, sinm_t, kv_g, w_kup, w_vup_t, w_alpha, b_alpha, *, seq):
    t, d = h.shape
    tm = DSA_KEY_CHUNK
    tiles = seq // tm
    gw = B_HEADS_PER_GROUP * HEAD_DIM
    row = lambda i: (i, 0)
    out_defs = [("q_a", 512, BF16), ("k_a", 512, BF16), ("v_t", "vt", BF16), ("q_i", 512, BF16),
                ("k_i", 128, BF16), ("small", LANES, F32)]
    out_defs += [(f"b_{nm}{gi}", ("dil", dil), BF16) for nm in "qkv" for gi, (_, dil) in enumerate(B_GROUPS)]
    out_defs += [("c_q", 256, F32), ("c_k", 256, F32), ("c_v", 512, F32), ("c_g", 512, F32),
                 ("la", 256, F32)]
    out_specs, out_shape = [], []
    for _, n, dt in out_defs:
        if n == "vt":
            out_specs.append(pl.BlockSpec((1, w_vup_t.shape[0], tm), lambda i: (i, 0, 0)))
            out_shape.append(jax.ShapeDtypeStruct((t // tm, w_vup_t.shape[0], tm), dt))
        elif isinstance(n, tuple):
            dil = n[1]
            out_specs.append(pl.BlockSpec((1, dil, tm // dil, gw),
                                          lambda i: (i // tiles, 0, i % tiles, 0)))
            out_shape.append(jax.ShapeDtypeStruct((t // seq, dil, seq // dil, gw), dt))
        else:
            out_specs.append(pl.BlockSpec((tm, n), row))
            out_shape.append(jax.ShapeDtypeStruct((t, n), dt))
    w_vup = w_vup_t
    outs = pl.pallas_call(
        _mixer_in_body,
        grid=(t // tm,),
        in_specs=[pl.BlockSpec((tm, d), row), _resident((1, d)), _resident(w_big.shape),
                  pl.BlockSpec((tm, LANES), row), pl.BlockSpec((tm, LANES), row),
                  pl.BlockSpec((tm, LANES), row), _resident(kv_g.shape), _resident(w_kup.shape),
                  _resident(w_vup.shape), _resident(w_alpha.shape), _resident(b_alpha.shape)],
        out_specs=out_specs,
        out_shape=out_shape,
        scratch_shapes=[pltpu.VMEM((B_HEADS * HEAD_DIM // LANES, tm, LANES), F32)],
        compiler_params=_params(1),
        name="mixer_in",
    )(h, g, w_big, cos_t, sinp_t, sinm_t, kv_g, w_kup, w_vup, w_alpha, b_alpha)
    return {name: o for (name, _, _), o in zip(out_defs, outs)}


def _masked_pair_rows(x, n_heads):
    out = []
    for h in range(n_heads):
        pair = x[:, (h // 2) * LANES:(h // 2 + 1) * LANES]
        out.append(jnp.where(_half_mask(pair.shape, h % 2 == 1), pair, jnp.zeros_like(pair)))
    return out


def _dsa_body(qi_ref, small_ref, qa_ref, ki_ref, ka_ref, vt_ref, o_ref,
              score_ref, acc_ref, qm_ref, s0_ref, s1_ref, *, seq, top_k):
    kc = DSA_KEY_CHUNK
    qb = DSA_Q_BLOCK
    i = pl.program_id(1)
    n_chunks = (i * qb + qb + kc - 1) // kc
    q_pos = i * qb + lax.broadcasted_iota(jnp.int32, (1, qb), 1)
    key_off = lax.broadcasted_iota(jnp.int32, (kc, qb), 0)

    n_pairs = A_HEADS // 2
    qi_heads = _masked_pair_rows(qi_ref[0], IDX_HEADS)
    for pr in range(n_pairs):
        qm_ref[pr] = jnp.concatenate([qi_heads[2 * pr], qi_heads[2 * pr + 1]], axis=0)
    w_t = jnp.transpose(small_ref[0])[SMALL_IW:SMALL_IW + IDX_HEADS, :] * (IDX_HEADS ** -0.5)

    def skewed(first, stage, carry):
        def step(j, st):
            carry, aux0 = st
            aux1 = first(2 * j + 1, s1_ref)
            carry = stage(2 * j, s0_ref, carry, aux0)
            aux0 = first(2 * j + 2, s0_ref)
            return stage(2 * j + 1, s1_ref, carry, aux1), aux0
        return lax.fori_loop(0, (n_chunks + 1) // 2, step, (carry, first(0, s0_ref)))[0]

    def idx_logits_stage(c, s_ref):
        off = pl.multiple_of(jnp.minimum(c, n_chunks - 1) * kc, kc)
        k_c = ki_ref[0, pl.ds(off, kc), :]
        for pr in range(n_pairs):
            s_ref[pr] = _dot_nt(k_c, qm_ref[pr])
        return 0

    def score_stage(c, s_ref, carry, _):
        mx, mn = carry
        cc = jnp.minimum(c, n_chunks - 1)
        sc = jnp.zeros((kc, qb), F32)
        for h in range(IDX_HEADS):
            logit = s_ref[h // 2, :, (h % 2) * qb:(h % 2 + 1) * qb]
            sc = sc + jnp.maximum(logit, 0.0) * w_t[h:h + 1, :]
        causal = (cc * kc + key_off) <= q_pos
        score_ref[cc] = jnp.where(causal, sc, -jnp.inf)
        mx = jnp.maximum(mx, jnp.max(jnp.where(causal, sc, -jnp.inf), axis=0, keepdims=True))
        mn = jnp.minimum(mn, jnp.min(jnp.where(causal, sc, jnp.inf), axis=0, keepdims=True))
        return mx, mn

    row_max, row_min = skewed(
        idx_logits_stage, score_stage,
        (jnp.full((1, qb), -jnp.inf, F32), jnp.full((1, qb), jnp.inf, F32)))

    slab = kc // COUNT_SLABS

    def sweep(init, fold, per_slab):
        def step(c, acc):
            for j in range(COUNT_SLABS):
                acc = fold(acc, per_slab(score_ref[c, j * slab:(j + 1) * slab, :]))
            return acc
        return lax.fori_loop(0, n_chunks, step, jnp.full((slab, qb), init, F32))

    def count_ge(thr):
        acc = sweep(0.0, lambda a, v: a + v, lambda x: jnp.where(x >= thr, 1.0, 0.0))
        return jnp.sum(acc, axis=0, keepdims=True)

    def max_below(thr):
        acc = sweep(-jnp.inf, jnp.maximum, lambda x: jnp.where(x < thr, x, -jnp.inf))
        return jnp.max(acc, axis=0, keepdims=True)

    k_f = float(top_k)
    n_causal = (q_pos + 1).astype(F32)
    few = n_causal <= k_f
    lo0 = row_min
    hi0 = row_max + (jnp.abs(row_max) + 1.0) * HI_MARGIN

    def halve(st):
        lo, hi, c_lo, c_hi = st
        mid = lo + (hi - lo) * 0.5
        cnt = count_ge(mid)
        ge = cnt >= k_f
        return (jnp.where(ge, mid, lo), jnp.where(ge, hi, mid),
                jnp.where(ge, cnt, c_lo), jnp.where(ge, c_hi, cnt))

    def snap(st):
        lo, hi, c_lo, c_hi = st
        v = max_below(hi)
        cnt = count_ge(v)
        ge = jnp.logical_and(cnt >= k_f, jnp.logical_not(few))
        lower = jnp.logical_and(cnt < k_f, jnp.logical_not(few))
        return (jnp.where(ge, v, lo), jnp.where(lower, v, hi),
                jnp.where(ge, cnt, c_lo), jnp.where(lower, cnt, c_hi)), ge

    st0 = (lo0, hi0, n_causal, jnp.zeros((1, qb), F32))
    st1 = lax.fori_loop(0, BISECT_WARMUP, lambda _, st: halve(st), st0)

    def search_cond(carry):
        return jnp.logical_and(carry[0] < BISECT_MAX_ITERS, carry[2] > 0.5)

    def search_step(carry):
        it, st, _ = carry
        for _ in range(BISECT_UNROLL):
            st = halve(st)
        st, snapped = snap(st)
        done = few | (st[2] == k_f) | snapped
        return it + 1, st, jnp.max(jnp.where(done, 0.0, 1.0))

    pending0 = jnp.max(jnp.where(few | (st1[2] == k_f), 0.0, 1.0))
    _, (lo, hi, c_lo, c_hi), _ = lax.while_loop(search_cond, search_step, (jnp.int32(0), st1, pending0))

    tied = jnp.logical_and(jnp.logical_not(few), c_lo > k_f)
    need = k_f - c_hi

    def tie_cut(_):
        r_i = lax.broadcasted_iota(jnp.int32, (kc, kc), 0)
        c_i = lax.broadcasted_iota(jnp.int32, (kc, kc), 1)
        tri = jnp.where(r_i >= c_i, 1.0, 0.0).astype(BF16)

        def step(c, st):
            seen, j_best = st
            x = score_ref[c]
            tie = (x >= lo) & (x < hi)
            prefix = seen + _dot(tri, jnp.where(tie, 1.0, 0.0).astype(BF16))
            idx = (c * kc + key_off).astype(F32)
            kept = jnp.where(tie & (prefix <= need), idx, -1.0)
            return prefix[kc - 1:kc, :], jnp.maximum(j_best, jnp.max(kept, axis=0, keepdims=True))
        return lax.fori_loop(0, n_chunks, step,
                             (jnp.zeros((1, qb), F32), jnp.full((1, qb), -1.0, F32)))[1]

    any_tied = jnp.max(jnp.where(tied, 1.0, 0.0)) > 0.5
    j_cut = lax.cond(any_tied, tie_cut, lambda _: jnp.full((1, qb), float(seq), F32), 0)
    j_cut = jnp.where(tied, j_cut, float(seq))
    thr_hi = jnp.where(tied, hi, lo)

    qa_heads = _masked_pair_rows(qa_ref[0], A_HEADS)
    for pr in range(n_pairs):
        qm_ref[pr] = jnp.concatenate([qa_heads[2 * pr], qa_heads[2 * pr + 1]], axis=0)
    acc_ref[...] = jnp.zeros(acc_ref.shape, F32)

    def attn_logits_stage(c, s_ref):
        cc = jnp.minimum(c, n_chunks - 1)
        off = pl.multiple_of(cc * kc, kc)
        x = score_ref[cc]
        sel = (x >= thr_hi) | ((x >= lo) & ((cc * kc + key_off).astype(F32) <= j_cut))
        bias = jnp.where(sel & (c < n_chunks), 0.0, MASKED_LOGIT)
        col_max = []
        for pr in range(n_pairs):
            s2 = _dot_nt(ka_ref[0, pl.ds(off, kc), pr * LANES:(pr + 1) * LANES], qm_ref[pr])
            for e in range(2):
                s = s2[:, e * qb:(e + 1) * qb] + bias
                s_ref[pr, :, e * qb:(e + 1) * qb] = s
                col_max.append(jnp.max(s, axis=0, keepdims=True))
        return tuple(col_max)

    def softmax_pv_stage(c, s_ref, m_all, col_max):
        cc = jnp.minimum(c, n_chunks - 1)
        m_out = []
        for pr in range(n_pairs):
            p_halves, alphas = [], []
            for e in range(2):
                h = 2 * pr + e
                m_new = jnp.maximum(m_all[h], col_max[h])
                alphas.append(jnp.exp2(m_all[h] - m_new))
                p_halves.append(jnp.exp2(s_ref[pr, :, e * qb:(e + 1) * qb] - m_new).astype(BF16))
                m_out.append(m_new)
            v_c = vt_ref[0, cc, pr * VT_ROWS:(pr + 1) * VT_ROWS, :]
            pv = _dot(v_c, jnp.concatenate(p_halves, axis=1))
            acc_ref[pr] = acc_ref[pr] * jnp.concatenate(alphas, axis=1) + pv
        return tuple(m_out)

    m0 = tuple(jnp.full((1, qb), MASKED_LOGIT, F32) for _ in range(A_HEADS))
    skewed(attn_logits_stage, softmax_pv_stage, m0)

    for pr in range(n_pairs):
        a = acc_ref[pr]
        denom = a[LANES:LANES + 1, :]
        o_t = jnp.concatenate([a[:HEAD_DIM, :qb] / denom[:, :qb],
                               a[HEAD_DIM:LANES, qb:] / denom[:, qb:]], axis=0)
        o_ref[0, :, pr * LANES:(pr + 1) * LANES] = jnp.transpose(o_t).astype(BF16)


def _dsa(q_i, small, q_a, k_i, k_a, v_t, *, top_k):
    bsz, seq, _ = q_a.shape
    qb, kc = DSA_Q_BLOCK, DSA_KEY_CHUNK
    blk = lambda b, i: (b, i, 0)
    whole = lambda shape: pl.BlockSpec((1,) + shape, lambda b, i: (b,) + (0,) * len(shape),
                                       pipeline_mode=pl.Buffered(1))
    return pl.pallas_call(
        functools.partial(_dsa_body, seq=seq, top_k=top_k),
        grid=(bsz, seq // qb),
        in_specs=[pl.BlockSpec((1, qb, 512), blk), pl.BlockSpec((1, qb, LANES), blk),
                  pl.BlockSpec((1, qb, 512), blk), whole((seq, LANES)), whole((seq, 512)),
                  whole(v_t.shape[1:])],
        out_specs=pl.BlockSpec((1, qb, 512), blk),
        out_shape=jax.ShapeDtypeStruct((bsz, seq, 512), BF16),
        scratch_shapes=[pltpu.VMEM((seq // kc, kc, qb), F32),
                        pltpu.VMEM((A_HEADS // 2, VT_ROWS, 2 * qb), F32),
                        pltpu.VMEM((A_HEADS // 2, 2 * qb, LANES), BF16),
                        pltpu.VMEM((A_HEADS // 2, kc, 2 * qb), F32),
                        pltpu.VMEM((A_HEADS // 2, kc, 2 * qb), F32)],
        compiler_params=_params(2),
        name="dsa",
    )(q_i, small, q_a, k_i, k_a, v_t)


def _dilated_body(q_ref, kp_ref, k_ref, vp_ref, v_ref, o_ref, lse_ref, *, tq):
    qb = Q_BLOCK
    n = pl.program_id(2)
    row = lax.broadcasted_iota(jnp.int32, (qb, 2 * qb), 0)
    col = lax.broadcasted_iota(jnp.int32, (qb, 2 * qb), 1)
    rel = row + qb - col
    band = (rel >= 0) & (rel <= qb)
    n_pairs = B_HEADS_PER_GROUP // 2
    for s in range(tq // qb):
        q = q_ref[0, 0, s * qb:(s + 1) * qb, :]
        if s == 0:
            k_prev, v_prev = kp_ref[0, 0], vp_ref[0, 0]
            valid = band & ((col >= qb) | (n > 0))
        else:
            k_prev = k_ref[0, 0, (s - 1) * qb:s * qb, :]
            v_prev = v_ref[0, 0, (s - 1) * qb:s * qb, :]
            valid = band
        kk = jnp.concatenate([k_prev, k_ref[0, 0, s * qb:(s + 1) * qb, :]], axis=0)
        vv = jnp.concatenate([v_prev, v_ref[0, 0, s * qb:(s + 1) * qb, :]], axis=0)
        o_parts, lse_parts = [], []
        for pr in range(n_pairs):
            q_pair = q[:, pr * LANES:(pr + 1) * LANES]
            k_pair = kk[:, pr * LANES:(pr + 1) * LANES]
            v_pair = vv[:, pr * LANES:(pr + 1) * LANES]
            o_h, lse_h = [], []
            for odd in (False, True):
                q_h = jnp.where(_half_mask(q_pair.shape, odd), q_pair, jnp.zeros_like(q_pair))
                sc = jnp.where(valid, _dot_nt(q_h, k_pair), -jnp.inf)
                m = jnp.max(sc, axis=1, keepdims=True)
                p = jnp.exp(sc - m)
                l = jnp.sum(p, axis=1, keepdims=True)
                o_h.append(_dot(p.astype(BF16), v_pair) / l)
                lse_h.append(jnp.broadcast_to(m + jnp.log(l), (qb, LANES)))
            lower = _half_mask((qb, LANES), False)
            o_parts.append(jnp.where(lower, o_h[0], o_h[1]))
            lse_parts.append(jnp.where(lower, lse_h[0], lse_h[1]))
        o_ref[0, 0, s * qb:(s + 1) * qb, :] = jnp.concatenate(o_parts, axis=1)
        lse_ref[0, 0, s * qb:(s + 1) * qb, :] = jnp.concatenate(lse_parts, axis=1)


def _dilated_group(b_q, b_k, b_v):
    bsz, dil, sub, gw = b_q.shape
    qb = Q_BLOCK
    tq = min(512, sub)
    cur = lambda b, r, n: (b, r, n, 0)
    prev = lambda b, r, n: (b, r, jnp.maximum(n * (tq // qb) - 1, 0), 0)
    return pl.pallas_call(
        functools.partial(_dilated_body, tq=tq),
        grid=(bsz, dil, sub // tq),
        in_specs=[pl.BlockSpec((1, 1, tq, gw), cur),
                  pl.BlockSpec((1, 1, qb, gw), prev), pl.BlockSpec((1, 1, tq, gw), cur),
                  pl.BlockSpec((1, 1, qb, gw), prev), pl.BlockSpec((1, 1, tq, gw), cur)],
        out_specs=[pl.BlockSpec((1, 1, tq, gw), cur), pl.BlockSpec((1, 1, tq, gw), cur)],
        out_shape=[jax.ShapeDtypeStruct(b_q.shape, F32)] * 2,
        compiler_params=_params(3),
        name=f"dilated_d{dil}",
    )(b_q, b_k, b_k, b_v, b_v)


def _gla_body(q_ref, k_ref, v_ref, g_ref, la_ref, ng_ref, o_ref,
              st_ref, b_ref, qh_ref, kd_ref, ke_ref, dec_ref, at_ref, oi_ref, kv_ref, *, n_sub):
    cs = C_CHUNK
    bsz = q_ref.shape[0]

    @pl.when(pl.program_id(0) == 0)
    def _():
        st_ref[...] = jnp.zeros(st_ref.shape, F32)

    r_i = lax.broadcasted_iota(jnp.int32, (cs, cs), 0)
    c_i = lax.broadcasted_iota(jnp.int32, (cs, cs), 1)
    tril = r_i >= c_i
    tri_ones = jnp.where(tril, 1.0, 0.0).astype(BF16)
    ng = ng_ref[...]
    units = [(c, bi) for c in range(n_sub) for bi in range(bsz)]
    rows_of = lambda c: slice(c * cs, (c + 1) * cs)
    pair_of = lambda h: slice((h // 2) * LANES, (h // 2 + 1) * LANES)
    v_of = lambda h: slice(h * C_HEAD_V, (h + 1) * C_HEAD_V)

    for u, (c, bi) in enumerate(units):
        la = la_ref[bi, rows_of(c), :]
        la_hi = la.astype(BF16)
        la_lo = (la - la_hi.astype(F32)).astype(BF16)
        b_ref[u] = _dot(tri_ones, la_hi) + _dot(tri_ones, la_lo)
    for u, (c, bi) in enumerate(units):
        b = b_ref[u]
        b_last = b[cs - 1:cs, :]
        k = k_ref[bi, rows_of(c), :]
        q_dec = q_ref[bi, rows_of(c), :] * jnp.exp(b)
        for h in range(C_HEADS):
            q_pair = q_dec[:, pair_of(h)]
            qh_ref[u * C_HEADS + h] = jnp.where(_half_mask(q_pair.shape, h % 2 == 1), q_pair,
                                                0.0).astype(BF16)
        kd_ref[u] = (k * jnp.exp(-b)).astype(BF16)
        ke_ref[u] = (k * jnp.exp(b_last - b)).astype(BF16)
        dec_ref[u] = jnp.exp(b_last)
    for u, (c, bi) in enumerate(units):
        for h in range(C_HEADS):
            attn = jnp.where(tril, _dot_nt(qh_ref[u * C_HEADS + h], kd_ref[u, :, pair_of(h)]), 0.0)
            at_ref[u * C_HEADS + h] = attn.astype(BF16)
    for u, (c, bi) in enumerate(units):
        for h in range(C_HEADS):
            v_h = v_ref[bi, rows_of(c), v_of(h)].astype(BF16)
            oi_ref[u * C_HEADS + h] = _dot(at_ref[u * C_HEADS + h], v_h)
            kv_ref[u * C_HEADS + h] = _dot_tn(v_h, ke_ref[u, :, pair_of(h)])
    for u, (c, bi) in enumerate(units):
        o_heads = []
        for h in range(C_HEADS):
            st = st_ref[bi * C_HEADS + h]
            o_h = oi_ref[u * C_HEADS + h] + _dot_nt(qh_ref[u * C_HEADS + h], st.astype(BF16))
            st_ref[bi * C_HEADS + h] = st * dec_ref[u, :, pair_of(h)] + kv_ref[u * C_HEADS + h]
            gate = g_ref[bi, rows_of(c), v_of(h)]
            o_heads.append(_rms(o_h, ng) * _silu(gate))
        o_ref[bi, rows_of(c), :] = jnp.concatenate(o_heads, axis=1).astype(BF16)


def _gla(c_q, c_k, c_v, c_g, la, norm_g, *, n_sub=8):
    bsz, seq, _ = c_q.shape
    rows = n_sub * C_CHUNK
    n_unit = n_sub * bsz
    blk = lambda n: (0, n, 0)
    kw, vw = C_HEADS * C_HEAD_K, C_HEADS * C_HEAD_V
    return pl.pallas_call(
        functools.partial(_gla_body, n_sub=n_sub),
        grid=(seq // rows,),
        in_specs=[pl.BlockSpec((bsz, rows, kw), blk), pl.BlockSpec((bsz, rows, kw), blk),
                  pl.BlockSpec((bsz, rows, vw), blk), pl.BlockSpec((bsz, rows, vw), blk),
                  pl.BlockSpec((bsz, rows, kw), blk), _resident(norm_g.shape)],
        out_specs=pl.BlockSpec((bsz, rows, vw), blk),
        out_shape=jax.ShapeDtypeStruct((bsz, seq, vw), BF16),
        scratch_shapes=[pltpu.VMEM((bsz * C_HEADS, C_HEAD_V, LANES), F32),
                        pltpu.VMEM((n_unit, C_CHUNK, kw), F32),
                        pltpu.VMEM((n_unit * C_HEADS, C_CHUNK, LANES), BF16),
                        pltpu.VMEM((n_unit, C_CHUNK, kw), BF16),
                        pltpu.VMEM((n_unit, C_CHUNK, kw), BF16),
                        pltpu.VMEM((n_unit, 1, kw), F32),
                        pltpu.VMEM((n_unit * C_HEADS, C_CHUNK, C_CHUNK), BF16),
                        pltpu.VMEM((n_unit * C_HEADS, C_CHUNK, C_HEAD_V), F32),
                        pltpu.VMEM((n_unit * C_HEADS, C_HEAD_V, LANES), F32)],
        compiler_params=_params(1),
        name="gla",
    )(c_q, c_k, c_v, c_g, la, norm_g)


def _mixer_out_body(h_ref, ya_ref, ob0_ref, ob1_ref, ob2_ref, ls0_ref, ls1_ref, ls2_ref, yc_ref,
                    pre_g_ref, wg_ref, bg_ref, wa_ref, wb_ref, wc_ref, wo_ref, post_g_ref, o_ref,
                    stage_ref):
    x = h_ref[...]
    tm, d = x.shape
    u = _rms(x, pre_g_ref[...]).astype(BF16)

    def token_order(blk_ref, slot):
        dil = blk_ref.shape[1]
        if dil == 1:
            return blk_ref[0, 0]
        n_l = blk_ref.shape[3] // LANES
        for r in range(dil):
            for j in range(n_l):
                stage_ref[slot * n_l + j, pl.ds(r, tm // dil, stride=dil), :] = (
                    blk_ref[0, r, :, j * LANES:(j + 1) * LANES])
        return jnp.concatenate([stage_ref[slot * n_l + j] for j in range(n_l)], axis=1)

    ls = [token_order(r, j) for j, r in enumerate((ls0_ref, ls1_ref, ls2_ref))]
    ob = [token_order(r, 3 + j) for j, r in enumerate((ob0_ref, ob1_ref, ob2_ref))]
    mx = jnp.maximum(jnp.maximum(ls[0], ls[1]), ls[2])
    ew = [jnp.exp(v - mx) for v in ls]
    y_b = ((ew[0] * ob[0] + ew[1] * ob[1] + ew[2] * ob[2]) / (ew[0] + ew[1] + ew[2])).astype(BF16)

    branches = ((ya_ref[...], wa_ref), (y_b, wb_ref), (yc_ref[...], wc_ref))
    merged = jnp.zeros(x.shape, F32)
    for i, (y, w_ref) in enumerate(branches):
        gate = jax.nn.sigmoid(_dot(u, wg_ref[:, i * d:(i + 1) * d]) + bg_ref[:, i * d:(i + 1) * d])
        merged = merged + gate * _dot(y, w_ref[...])
    m = _dot(merged.astype(BF16), wo_ref[...])
    o_ref[...] = x + _rms(m, post_g_ref[...])


def _mixer_out(h, y_a, o_b, lse_b, y_c, pre_g, w_gate, b_gate, w_a, w_b, w_c, w_o, post_g, *, seq,
               tm=512):
    t, d = h.shape
    tiles = seq // tm
    row = lambda i: (i, 0)

    def tile(a):
        if a.ndim == 2:
            return pl.BlockSpec((tm, a.shape[1]), row)
        dil, gw = a.shape[1], a.shape[3]
        return pl.BlockSpec((1, dil, tm // dil, gw), lambda i: (i // tiles, 0, i % tiles, 0))

    acts = [h, y_a, *o_b, *lse_b, y_c]
    consts = [pre_g, w_gate, b_gate, w_a, w_b, w_c, w_o, post_g]
    return pl.pallas_call(
        _mixer_out_body,
        grid=(t // tm,),
        in_specs=[tile(a) for a in acts] + [_resident(c.shape) for c in consts],
        out_specs=pl.BlockSpec((tm, d), row),
        out_shape=jax.ShapeDtypeStruct((t, d), F32),
        scratch_shapes=[pltpu.VMEM((2 * len(o_b) * o_b[0].shape[3] // LANES, tm, LANES), F32)],
        compiler_params=_params(1),
        name="mixer_out",
    )(*acts, *consts)


def _rope_tables(positions):
    half = ROT_DIM // 2
    inv_freq = ROPE_THETA ** (-jnp.arange(0, ROT_DIM, 2, dtype=F32) / ROT_DIM)
    ang = positions.astype(F32).reshape(-1, 1) * inv_freq
    cos, sin = jnp.cos(ang), jnp.sin(ang)
    t = cos.shape[0]
    pad = lambda n: jnp.zeros((t, n), F32)
    cos_t = jnp.concatenate([cos, cos, jnp.ones((t, HEAD_DIM - 2 * half), F32)], axis=1)
    sinp_t = jnp.concatenate([pad(half), sin, pad(HEAD_DIM - 2 * half)], axis=1)
    sinm_t = jnp.concatenate([-sin, pad(HEAD_DIM - half)], axis=1)
    rep = LANES // HEAD_DIM
    return tuple(jnp.tile(a, (1, rep)) for a in (cos_t, sinp_t, sinm_t))


def _split_w_in(w_in):
    d = w_in.shape[0]
    sizes = (A_HEADS * HEAD_DIM, A_KV_RANK, IDX_HEADS * IDX_DIM, IDX_DIM, IDX_HEADS,
             3 * B_HEADS * HEAD_DIM, C_HEADS * C_HEAD_K, C_HEADS * C_HEAD_K,
             C_HEADS * C_HEAD_V, C_HEADS * C_HEAD_V, C_ALPHA_RANK)
    names = ("a_q", "ckv", "i_q", "i_k", "i_w", "b_qkv", "c_q", "c_k", "c_v", "c_g", "c_a")
    parts, o = {}, 0
    for name, n in zip(names, sizes):
        parts[name] = w_in[:, o:o + n]
        o += n
    bw = B_HEADS * HEAD_DIM
    parts["b_q"], parts["b_k"], parts["b_v"] = (parts["b_qkv"][:, j * bw:(j + 1) * bw] for j in range(3))
    used = IDX_DIM + IDX_HEADS + C_ALPHA_RANK
    parts["small"] = jnp.concatenate(
        [parts["i_k"], parts["i_w"], parts["c_a"], jnp.zeros((d, LANES - used), w_in.dtype)], axis=1)
    return jnp.concatenate([parts[name] for name, _ in _IN_COLS], axis=1).astype(BF16)


def kernel(x, positions, ffn1_pre_g, ffn1_w_in, ffn1_w_out, ffn1_post_g, mix_pre_g, w_in, a_kv_norm_g, a_w_kv_up, c_w_alpha_up, c_b_alpha, c_norm_g, w_branch_a, w_branch_b, w_branch_c, w_gate, b_gate, w_out, mix_post_g, ffn2_pre_g, ffn2_w_in, ffn2_w_out, ffn2_post_g):
    bsz, seq, d = x.shape
    depth = w_in.shape[0]
    top_k = min(TOPK_MAX, seq // 4)
    tables = _rope_tables(positions)
    row = lambda a: a.reshape(1, -1)
    h = x.reshape(bsz * seq, d)
    for l in range(depth):
        h = _ffn(h, row(ffn1_pre_g[l]), ffn1_w_in[l].astype(BF16), ffn1_w_out[l].astype(BF16),
                 row(ffn1_post_g[l]))

        kv_up = a_w_kv_up[l].reshape(A_KV_RANK, A_HEADS, 2, HEAD_DIM)
        w_kup = kv_up[:, :, 0].reshape(A_KV_RANK, A_HEADS * HEAD_DIM).astype(BF16)
        w_vup_t = kv_up[:, :, 1].reshape(A_KV_RANK, A_HEADS // 2, LANES).transpose(1, 2, 0)
        w_vup_t = jnp.pad(w_vup_t, ((0, 0), (0, VT_ROWS - LANES), (0, 0)))
        w_vup_t = w_vup_t.reshape(A_HEADS // 2 * VT_ROWS, A_KV_RANK).astype(BF16)
        w_alpha = jnp.zeros((LANES, C_HEADS * C_HEAD_K), F32)
        w_alpha = w_alpha.at[SMALL_CA:SMALL_CA + C_ALPHA_RANK].set(c_w_alpha_up[l]).astype(BF16)
        p = _mixer_in(h, row(mix_pre_g[l]), _split_w_in(w_in[l]), *tables, row(a_kv_norm_g[l]),
                      w_kup, w_vup_t, w_alpha, row(c_b_alpha[l]), seq=seq)

        seq3 = lambda a: a.reshape(bsz, seq, a.shape[-1])
        v_t = p["v_t"].reshape(bsz, seq // DSA_KEY_CHUNK, -1, DSA_KEY_CHUNK)
        y_a = _dsa(seq3(p["q_i"]), seq3(p["small"]), seq3(p["q_a"]), seq3(p["k_i"]), seq3(p["k_a"]),
                   v_t, top_k=top_k).reshape(bsz * seq, -1)
        o_b, lse_b = [], []
        for g in range(len(B_GROUPS)):
            o_g, lse_g = _dilated_group(p[f"b_q{g}"], p[f"b_k{g}"], p[f"b_v{g}"])
            o_b.append(o_g)
            lse_b.append(lse_g)
        y_c = _gla(seq3(p["c_q"]), seq3(p["c_k"]), seq3(p["c_v"]), seq3(p["c_g"]), seq3(p["la"]),
                   row(c_norm_g[l])).reshape(bsz * seq, -1)

        h = _mixer_out(h, y_a, o_b, lse_b, y_c, row(mix_pre_g[l]), w_gate[l].astype(BF16),
                       row(b_gate[l]), w_branch_a[l].astype(BF16), w_branch_b[l].astype(BF16),
                       w_branch_c[l].astype(BF16), w_out[l].astype(BF16), row(mix_post_g[l]), seq=seq)

        h = _ffn(h, row(ffn2_pre_g[l]), ffn2_w_in[l].astype(BF16), ffn2_w_out[l].astype(BF16),
                 row(ffn2_post_g[l]))
    return h.reshape(bsz, seq, d)
```

```python
import functools

import jax
import jax.numpy as jnp
from jax import lax
from jax.experimental import pallas as pl
from jax.experimental.pallas import tpu as pltpu

F32 = jnp.float32
BF16 = jnp.bfloat16

HEAD_DIM = 64
ROT_DIM = HEAD_DIM // 4
ROPE_THETA = 500000.0
RMS_EPS = 1e-6
MACARON_WEIGHT = 0.5

A_HEADS = 8
A_KV_RANK = 128
IDX_HEADS = 8
IDX_DIM = 64
TOPK_MAX = 256

B_GROUPS = ((128, 1), (512, 4), (2048, 16))
B_HEADS_PER_GROUP = 4
B_HEADS = B_HEADS_PER_GROUP * len(B_GROUPS)

C_HEADS = 4
C_HEAD_K = 64
C_HEAD_V = 128
C_ALPHA_RANK = 16
C_GATE_TAU = 16.0
C_CHUNK = 64

LANES = 128
MXU_TILE = 256
Q_BLOCK = 128
DSA_Q_BLOCK = 256
DSA_KEY_CHUNK = 512
BF16_SUBLANES = 16
VT_ROWS = LANES + BF16_SUBLANES
BISECT_WARMUP = 14
HI_MARGIN = 2.0 ** -10
BISECT_UNROLL = 2
BISECT_MAX_ITERS = 16
COUNT_SLABS = 16
MASKED_LOGIT = -1e30
LOG2_E = 1.4426950408889634
VMEM_LIMIT = 60 * 1024 * 1024

SMALL_IK = 0
SMALL_IW = IDX_DIM
SMALL_CA = IDX_DIM + IDX_HEADS


def _dot(a, b):
    return jnp.dot(a, b, preferred_element_type=F32)


def _dot_nt(a, b):
    return lax.dot_general(a, b, (((1,), (1,)), ((), ())), preferred_element_type=F32)


def _dot_tn(a, b):
    return lax.dot_general(a, b, (((0,), (0,)), ((), ())), preferred_element_type=F32)


def _rms(x, g):
    return x * lax.rsqrt(jnp.mean(x * x, axis=-1, keepdims=True) + RMS_EPS) * g


def _silu(x):
    return x * jax.nn.sigmoid(x)


def _params(n_grid):
    return pltpu.CompilerParams(dimension_semantics=("arbitrary",) * n_grid,
                                vmem_limit_bytes=VMEM_LIMIT)


def _resident(shape):
    zeros = (0,) * len(shape)
    return pl.BlockSpec(shape, lambda *_: zeros, pipeline_mode=pl.Buffered(1))


def _half_mask(shape, upper):
    lane = lax.broadcasted_iota(jnp.int32, shape, len(shape) - 1) % LANES
    return (lane >= HEAD_DIM) if upper else (lane < HEAD_DIM)


def _ffn_body(h_ref, pre_g_ref, w_in_ref, w_out_ref, post_g_ref, o_ref, *, d_ff, cuts):
    x = h_ref[...]
    xn = _rms(x, pre_g_ref[...]).astype(BF16)
    acc = jnp.zeros(x.shape, F32)
    for lo, hi in zip(cuts[:-1], cuts[1:]):
        gate = _dot(xn, w_in_ref[:, lo:hi])
        up = _dot(xn, w_in_ref[:, d_ff + lo:d_ff + hi])
        act = (_silu(gate) * up).astype(BF16)
        acc = acc + _dot(act, w_out_ref[lo:hi, :])
    o_ref[...] = x + MACARON_WEIGHT * _rms(acc, post_g_ref[...])


def _ffn(h, pre_g, w_in, w_out, post_g, *, tm=512):
    t, d = h.shape
    d_ff = w_out.shape[0]
    mid = (d_ff // 2 + MXU_TILE - 1) // MXU_TILE * MXU_TILE
    cuts = (0, mid, d_ff) if 0 < mid < d_ff else (0, d_ff)
    row = lambda i: (i, 0)
    return pl.pallas_call(
        functools.partial(_ffn_body, d_ff=d_ff, cuts=cuts),
        grid=(t // tm,),
        in_specs=[pl.BlockSpec((tm, d), row), _resident((1, d)), _resident(w_in.shape),
                  _resident(w_out.shape), _resident((1, d))],
        out_specs=pl.BlockSpec((tm, d), row),
        out_shape=jax.ShapeDtypeStruct((t, d), F32),
        compiler_params=_params(1),
        name="ffn",
    )(h, pre_g, w_in, w_out, post_g)


_IN_COLS = (("a_q", 512), ("i_q", 512), ("b_q", 768), ("b_k", 768), ("b_v", 768),
            ("c_q", 256), ("c_k", 256), ("c_v", 512), ("c_g", 512), ("ckv", 128), ("small", 128))


def _in_col_offsets():
    offs, o = {}, 0
    for name, n in _IN_COLS:
        offs[name] = (o, o + n)
        o += n
    return offs, o


def _rope(x, cos_t, sinp_t, sinm_t):
    outs = []
    for i in range(x.shape[1] // LANES):
        xs = x[:, i * LANES:(i + 1) * LANES]
        outs.append(xs * cos_t + pltpu.roll(xs, 8, 1) * sinp_t + pltpu.roll(xs, LANES - 8, 1) * sinm_t)
    return outs[0] if len(outs) == 1 else jnp.concatenate(outs, axis=1)


def _log_sigmoid(x):
    return jnp.minimum(x, 0.0) - jnp.log1p(jnp.exp(-jnp.abs(x)))


def _mixer_in_body(h_ref, g_ref, w_ref, cos_ref, sinp_ref, sinm_ref, kvg_ref, wk_ref, wv_ref,
                   wal_ref, bal_ref,
                   qa_ref, ka_ref, vt_ref, qi_ref, ki_ref, small_ref,
                   bq0_ref, bq1_ref, bq2_ref, bk0_ref, bk1_ref, bk2_ref, bv0_ref, bv1_ref, bv2_ref,
                   cq_ref, ck_ref, cv_ref, cg_ref, la_ref, stage_ref):
    offs, _ = _in_col_offsets()
    u = _rms(h_ref[...], g_ref[...]).astype(BF16)
    cos_t, sinp_t, sinm_t = cos_ref[...], sinp_ref[...], sinm_ref[...]
    tm = h_ref.shape[0]
    gw = B_HEADS_PER_GROUP * HEAD_DIM

    def proj(name):
        lo, hi = offs[name]
        return _dot(u, w_ref[:, lo:hi])

    def emit_dilated(x, out_refs):
        for j in range(x.shape[1] // LANES):
            stage_ref[j] = x[:, j * LANES:(j + 1) * LANES]
        per_g = gw // LANES
        for g, (_, dil) in enumerate(B_GROUPS):
            for r in range(dil):
                rows = [stage_ref[g * per_g + j, pl.ds(r, tm // dil, stride=dil), :] for j in range(per_g)]
                out_refs[g][0, r] = jnp.concatenate(rows, axis=1).astype(BF16)

    rope = lambda x: _rope(x, cos_t, sinp_t, sinm_t)
    q_scale = HEAD_DIM ** -0.5
    qa_ref[...] = (rope(proj("a_q")) * (q_scale * LOG2_E)).astype(BF16)
    qi_ref[...] = (rope(proj("i_q")) * (IDX_DIM ** -0.5)).astype(BF16)
    emit_dilated(rope(proj("b_q")) * q_scale, (bq0_ref, bq1_ref, bq2_ref))
    emit_dilated(rope(proj("b_k")), (bk0_ref, bk1_ref, bk2_ref))
    emit_dilated(proj("b_v"), (bv0_ref, bv1_ref, bv2_ref))
    cq_ref[...] = proj("c_q") * (C_HEAD_K ** -0.5)
    ck_ref[...] = proj("c_k")
    cv_ref[...] = proj("c_v")
    cg_ref[...] = proj("c_g")

    ckv = _rms(proj("ckv"), kvg_ref[...]).astype(BF16)
    ka_ref[...] = rope(_dot(ckv, wk_ref[...])).astype(BF16)
    v_t = _dot_nt(wv_ref[...], ckv)
    slab_row = lax.broadcasted_iota(jnp.int32, v_t.shape, 0) % VT_ROWS
    vt_ref[0] = jnp.where(slab_row >= LANES, 1.0, v_t).astype(BF16)

    small = proj("small")
    small_r = rope(small)
    ki_ref[...] = jnp.where(_half_mask(small.shape, False), small_r,
                            pltpu.roll(small_r, HEAD_DIM, 1)).astype(BF16)
    small_ref[...] = small
    la = _dot(small.astype(BF16), wal_ref[...]) + bal_ref[...]
    la_ref[...] = _log_sigmoid(la) * (1.0 / C_GATE_TAU)


def _mixer_in(h, g, w_big, cos_t, sinp_t, sinm_t, kv_g, w_kup, w_vup_t, w_alpha, b_alpha, *, seq):
    t, d = h.shape
    tm = DSA_KEY_CHUNK
    tiles = seq // tm
    gw = B_HEADS_PER_GROUP * HEAD_DIM
    row = lambda i: (i, 0)
    out_defs = [("q_a", 512, BF16), ("k_a", 512, BF16), ("v_t", "vt", BF16), ("q_i", 512, BF16),
                ("k_i", 128, BF16), ("small", LANES, F32)]
    out_defs += [(f"b_{nm}{gi}", ("dil", dil), BF16) for nm in "qkv" for gi, (_, dil) in enumerate(B_GROUPS)]
    out_defs += [("c_q", 256, F32), ("c_k", 256, F32), ("c_v", 512, F32), ("c_g", 512, F32),
                 ("la", 256, F32)]
    out_specs, out_shape = [], []
    for _, n, dt in out_defs:
        if n == "vt":
            out_specs.append(pl.BlockSpec((1, w_vup_t.shape[0], tm), lambda i: (i, 0, 0)))
            out_shape.append(jax.ShapeDtypeStruct((t // tm, w_vup_t.shape[0], tm), dt))
        elif isinstance(n, tuple):
            dil = n[1]
            out_specs.append(pl.BlockSpec((1, dil, tm // dil, gw),
                                          lambda i: (i // tiles, 0, i % tiles, 0)))
            out_shape.append(jax.ShapeDtypeStruct((t // seq, dil, seq // dil, gw), dt))
        else:
            out_specs.append(pl.BlockSpec((tm, n), row))
            out_shape.append(jax.ShapeDtypeStruct((t, n), dt))
    w_vup = w_vup_t
    outs = pl.pallas_call(
        _mixer_in_body,
        grid=(t // tm,),
        in_specs=[pl.BlockSpec((tm, d), row), _resident((1, d)), _resident(w_big.shape),
                  pl.BlockSpec((tm, LANES), row), pl.BlockSpec((tm, LANES), row),
                  pl.BlockSpec((tm, LANES), row), _resident(kv_g.shape), _resident(w_kup.shape),
                  _resident(w_vup.shape), _resident(w_alpha.shape), _resident(b_alpha.shape)],
        out_specs=out_specs,
        out_shape=out_shape,
        scratch_shapes=[pltpu.VMEM((B_HEADS * HEAD_DIM // LANES, tm, LANES), F32)],
        compiler_params=_params(1),
        name="mixer_in",
    )(h, g, w_big, cos_t, sinp_t, sinm_t, kv_g, w_kup, w_vup, w_alpha, b_alpha)
    return {name: o for (name, _, _), o in zip(out_defs, outs)}


def _masked_pair_rows(x, n_heads):
    out = []
    for h in range(n_heads):
        pair = x[:, (h // 2) * LANES:(h // 2 + 1) * LANES]
        out.append(jnp.where(_half_mask(pair.shape, h % 2 == 1), pair, jnp.zeros_like(pair)))
    return out


def _dsa_body(qi_ref, small_ref, qa_ref, ki_ref, ka_ref, vt_ref, o_ref,
              score_ref, acc_ref, qm_ref, s0_ref, s1_ref, *, seq, top_k):
    kc = DSA_KEY_CHUNK
    qb = DSA_Q_BLOCK
    i = pl.program_id(1)
    n_chunks = (i * qb + qb + kc - 1) // kc
    q_pos = i * qb + lax.broadcasted_iota(jnp.int32, (1, qb), 1)
    key_off = lax.broadcasted_iota(jnp.int32, (kc, qb), 0)

    n_pairs = A_HEADS // 2
    qi_heads = _masked_pair_rows(qi_ref[0], IDX_HEADS)
    for pr in range(n_pairs):
        qm_ref[pr] = jnp.concatenate([qi_heads[2 * pr], qi_heads[2 * pr + 1]], axis=0)
    w_t = jnp.transpose(small_ref[0])[SMALL_IW:SMALL_IW + IDX_HEADS, :] * (IDX_HEADS ** -0.5)

    def skewed(first, stage, carry):
        def step(j, st):
            carry, aux0 = st
            aux1 = first(2 * j + 1, s1_ref)
            carry = stage(2 * j, s0_ref, carry, aux0)
            aux0 = first(2 * j + 2, s0_ref)
            return stage(2 * j + 1, s1_ref, carry, aux1), aux0
        return lax.fori_loop(0, (n_chunks + 1) // 2, step, (carry, first(0, s0_ref)))[0]

    def idx_logits_stage(c, s_ref):
        off = pl.multiple_of(jnp.minimum(c, n_chunks - 1) * kc, kc)
        k_c = ki_ref[0, pl.ds(off, kc), :]
        for pr in range(n_pairs):
            s_ref[pr] = _dot_nt(k_c, qm_ref[pr])
        return 0

    def score_stage(c, s_ref, carry, _):
        mx, mn = carry
        cc = jnp.minimum(c, n_chunks - 1)
        sc = jnp.zeros((kc, qb), F32)
        for h in range(IDX_HEADS):
            logit = s_ref[h // 2, :, (h % 2) * qb:(h % 2 + 1) * qb]
            sc = sc + jnp.maximum(logit, 0.0) * w_t[h:h + 1, :]
        causal = (cc * kc + key_off) <= q_pos
        score_ref[cc] = jnp.where(causal, sc, -jnp.inf)
        mx = jnp.maximum(mx, jnp.max(jnp.where(causal, sc, -jnp.inf), axis=0, keepdims=True))
        mn = jnp.minimum(mn, jnp.min(jnp.where(causal, sc, jnp.inf), axis=0, keepdims=True))
        return mx, mn

    row_max, row_min = skewed(
        idx_logits_stage, score_stage,
        (jnp.full((1, qb), -jnp.inf, F32), jnp.full((1, qb), jnp.inf, F32)))

    slab = kc // COUNT_SLABS

    def sweep(init, fold, per_slab):
        def step(c, acc):
            for j in range(COUNT_SLABS):
                acc = fold(acc, per_slab(score_ref[c, j * slab:(j + 1) * slab, :]))
            return acc
        return lax.fori_loop(0, n_chunks, step, jnp.full((slab, qb), init, F32))

    def count_ge(thr):
        acc = sweep(0.0, lambda a, v: a + v, lambda x: jnp.where(x >= thr, 1.0, 0.0))
        return jnp.sum(acc, axis=0, keepdims=True)

    def max_below(thr):
        acc = sweep(-jnp.inf, jnp.maximum, lambda x: jnp.where(x < thr, x, -jnp.inf))
        return jnp.max(acc, axis=0, keepdims=True)

    k_f = float(top_k)
    n_causal = (q_pos + 1).astype(F32)
    few = n_causal <= k_f
    lo0 = row_min
    hi0 = row_max + (jnp.abs(row_max) + 1.0) * HI_MARGIN

    def halve(st):
        lo, hi, c_lo, c_hi = st
        mid = lo + (hi - lo) * 0.5
        cnt = count_ge(mid)
        ge = cnt >= k_f
        return (jnp.where(ge, mid, lo), jnp.where(ge, hi, mid),
                jnp.where(ge, cnt, c_lo), jnp.where(ge, c_hi, cnt))

    def snap(st):
        lo, hi, c_lo, c_hi = st
        v = max_below(hi)
        cnt = count_ge(v)
        ge = jnp.logical_and(cnt >= k_f, jnp.logical_not(few))
        lower = jnp.logical_and(cnt < k_f, jnp.logical_not(few))
        return (jnp.where(ge, v, lo), jnp.where(lower, v, hi),
                jnp.where(ge, cnt, c_lo), jnp.where(lower, cnt, c_hi)), ge

    st0 = (lo0, hi0, n_causal, jnp.zeros((1, qb), F32))
    st1 = lax.fori_loop(0, BISECT_WARMUP, lambda _, st: halve(st), st0)

    def search_cond(carry):
        return jnp.logical_and(carry[0] < BISECT_MAX_ITERS, carry[2] > 0.5)

    def search_step(carry):
        it, st, _ = carry
        for _ in range(BISECT_UNROLL):
            st = halve(st)
        st, snapped = snap(st)
        done = few | (st[2] == k_f) | snapped
        return it + 1, st, jnp.max(jnp.where(done, 0.0, 1.0))

    pending0 = jnp.max(jnp.where(few | (st1[2] == k_f), 0.0, 1.0))
    _, (lo, hi, c_lo, c_hi), _ = lax.while_loop(search_cond, search_step, (jnp.int32(0), st1, pending0))

    tied = jnp.logical_and(jnp.logical_not(few), c_lo > k_f)
    need = k_f - c_hi

    @pl.when(jnp.max(jnp.where(tied, 1.0, 0.0)) > 0.5)
    def _():
        r_i = lax.broadcasted_iota(jnp.int32, (kc, kc), 0)
        c_i = lax.broadcasted_iota(jnp.int32, (kc, kc), 1)
        tri = jnp.where(r_i >= c_i, 1.0, 0.0).astype(BF16)

        def step(c, seen):
            x = score_ref[c]
            tie = tied & (x >= lo) & (x < hi)
            prefix = seen + _dot(tri, jnp.where(tie, 1.0, 0.0).astype(BF16))
            score_ref[c] = jnp.where(tie & (prefix > need), -jnp.inf, x)
            return prefix[kc - 1:kc, :]
        lax.fori_loop(0, n_chunks, step, jnp.zeros((1, qb), F32))

    qa_heads = _masked_pair_rows(qa_ref[0], A_HEADS)
    for pr in range(n_pairs):
        qm_ref[pr] = jnp.concatenate([qa_heads[2 * pr], qa_heads[2 * pr + 1]], axis=0)
    acc_ref[...] = jnp.zeros(acc_ref.shape, F32)

    def attn_logits_stage(c, s_ref):
        cc = jnp.minimum(c, n_chunks - 1)
        off = pl.multiple_of(cc * kc, kc)
        bias = jnp.where((score_ref[cc] >= lo) & (c < n_chunks), 0.0, MASKED_LOGIT)
        col_max = []
        for pr in range(n_pairs):
            s2 = _dot_nt(ka_ref[0, pl.ds(off, kc), pr * LANES:(pr + 1) * LANES], qm_ref[pr])
            for e in range(2):
                s = s2[:, e * qb:(e + 1) * qb] + bias
                s_ref[pr, :, e * qb:(e + 1) * qb] = s
                col_max.append(jnp.max(s, axis=0, keepdims=True))
        return tuple(col_max)

    def softmax_pv_stage(c, s_ref, m_all, col_max):
        cc = jnp.minimum(c, n_chunks - 1)
        m_out = []
        for pr in range(n_pairs):
            p_halves, alphas = [], []
            for e in range(2):
                h = 2 * pr + e
                m_new = jnp.maximum(m_all[h], col_max[h])
                alphas.append(jnp.exp2(m_all[h] - m_new))
                p_halves.append(jnp.exp2(s_ref[pr, :, e * qb:(e + 1) * qb] - m_new).astype(BF16))
                m_out.append(m_new)
            v_c = vt_ref[0, cc, pr * VT_ROWS:(pr + 1) * VT_ROWS, :]
            pv = _dot(v_c, jnp.concatenate(p_halves, axis=1))
            acc_ref[pr] = acc_ref[pr] * jnp.concatenate(alphas, axis=1) + pv
        return tuple(m_out)

    m0 = tuple(jnp.full((1, qb), MASKED_LOGIT, F32) for _ in range(A_HEADS))
    skewed(attn_logits_stage, softmax_pv_stage, m0)

    for pr in range(n_pairs):
        a = acc_ref[pr]
        denom = a[LANES:LANES + 1, :]
        o_t = jnp.concatenate([a[:HEAD_DIM, :qb] / denom[:, :qb],
                               a[HEAD_DIM:LANES, qb:] / denom[:, qb:]], axis=0)
        o_ref[0, :, pr * LANES:(pr + 1) * LANES] = jnp.transpose(o_t).astype(BF16)


def _dsa(q_i, small, q_a, k_i, k_a, v_t, *, top_k):
    bsz, seq, _ = q_a.shape
    qb, kc = DSA_Q_BLOCK, DSA_KEY_CHUNK
    blk = lambda b, i: (b, i, 0)
    whole = lambda shape: pl.BlockSpec((1,) + shape, lambda b, i: (b,) + (0,) * len(shape),
                                       pipeline_mode=pl.Buffered(1))
    return pl.pallas_call(
        functools.partial(_dsa_body, seq=seq, top_k=top_k),
        grid=(bsz, seq // qb),
        in_specs=[pl.BlockSpec((1, qb, 512), blk), pl.BlockSpec((1, qb, LANES), blk),
                  pl.BlockSpec((1, qb, 512), blk), whole((seq, LANES)), whole((seq, 512)),
                  whole(v_t.shape[1:])],
        out_specs=pl.BlockSpec((1, qb, 512), blk),
        out_shape=jax.ShapeDtypeStruct((bsz, seq, 512), BF16),
        scratch_shapes=[pltpu.VMEM((seq // kc, kc, qb), F32),
                        pltpu.VMEM((A_HEADS // 2, VT_ROWS, 2 * qb), F32),
                        pltpu.VMEM((A_HEADS // 2, 2 * qb, LANES), BF16),
                        pltpu.VMEM((A_HEADS // 2, kc, 2 * qb), F32),
                        pltpu.VMEM((A_HEADS // 2, kc, 2 * qb), F32)],
        compiler_params=_params(2),
        name="dsa",
    )(q_i, small, q_a, k_i, k_a, v_t)


def _dilated_body(q_ref, kp_ref, k_ref, vp_ref, v_ref, o_ref, lse_ref,
                  kk_ref, vv_ref, sc_ref, p_ref, stat_ref, *, tq):
    qb = Q_BLOCK
    n = pl.program_id(2)
    row = lax.broadcasted_iota(jnp.int32, (qb, 2 * qb), 0)
    col = lax.broadcasted_iota(jnp.int32, (qb, 2 * qb), 1)
    rel = row + qb - col
    band = (rel >= 0) & (rel <= qb)
    n_blk = tq // qb
    heads = B_HEADS_PER_GROUP
    pair_of = lambda h: slice((h // 2) * LANES, (h // 2 + 1) * LANES)

    for s in range(n_blk):
        rows = slice(s * qb, (s + 1) * qb)
        prev_rows = slice((s - 1) * qb, s * qb)
        kk_ref[s, :qb] = kp_ref[0, 0] if s == 0 else k_ref[0, 0, prev_rows, :]
        kk_ref[s, qb:] = k_ref[0, 0, rows, :]
        vv_ref[s, :qb] = vp_ref[0, 0] if s == 0 else v_ref[0, 0, prev_rows, :]
        vv_ref[s, qb:] = v_ref[0, 0, rows, :]
    for s in range(n_blk):
        valid = (band & ((col >= qb) | (n > 0))) if s == 0 else band
        for h in range(heads):
            q_pair = q_ref[0, 0, s * qb:(s + 1) * qb, pair_of(h)]
            q_h = jnp.where(_half_mask(q_pair.shape, h % 2 == 1), q_pair, jnp.zeros_like(q_pair))
            sc_ref[s * heads + h] = jnp.where(valid, _dot_nt(q_h, kk_ref[s, :, pair_of(h)]), -jnp.inf)
    for u in range(n_blk * heads):
        sc = sc_ref[u]
        m = jnp.max(sc, axis=1, keepdims=True)
        p = jnp.exp(sc - m)
        l = jnp.sum(p, axis=1, keepdims=True)
        p_ref[u] = p.astype(BF16)
        stat_ref[2 * u] = jnp.broadcast_to(l, (qb, LANES))
        stat_ref[2 * u + 1] = jnp.broadcast_to(m + jnp.log(l), (qb, LANES))
    lower = _half_mask((qb, LANES), False)
    for s in range(n_blk):
        o_parts, lse_parts = [], []
        for pr in range(heads // 2):
            o_h = []
            for h in (2 * pr, 2 * pr + 1):
                u = s * heads + h
                o_h.append(_dot(p_ref[u], vv_ref[s, :, pair_of(h)]) / stat_ref[2 * u])
            o_parts.append(jnp.where(lower, o_h[0], o_h[1]))
            lse_parts.append(jnp.where(lower, stat_ref[2 * (s * heads + 2 * pr) + 1],
                                       stat_ref[2 * (s * heads + 2 * pr + 1) + 1]))
        o_ref[0, 0, s * qb:(s + 1) * qb, :] = jnp.concatenate(o_parts, axis=1)
        lse_ref[0, 0, s * qb:(s + 1) * qb, :] = jnp.concatenate(lse_parts, axis=1)


def _dilated_group(b_q, b_k, b_v):
    bsz, dil, sub, gw = b_q.shape
    qb = Q_BLOCK
    tq = min(512, sub)
    n_unit = tq // qb * B_HEADS_PER_GROUP
    cur = lambda b, r, n: (b, r, n, 0)
    prev = lambda b, r, n: (b, r, jnp.maximum(n * (tq // qb) - 1, 0), 0)
    return pl.pallas_call(
        functools.partial(_dilated_body, tq=tq),
        grid=(bsz, dil, sub // tq),
        in_specs=[pl.BlockSpec((1, 1, tq, gw), cur),
                  pl.BlockSpec((1, 1, qb, gw), prev), pl.BlockSpec((1, 1, tq, gw), cur),
                  pl.BlockSpec((1, 1, qb, gw), prev), pl.BlockSpec((1, 1, tq, gw), cur)],
        out_specs=[pl.BlockSpec((1, 1, tq, gw), cur), pl.BlockSpec((1, 1, tq, gw), cur)],
        out_shape=[jax.ShapeDtypeStruct(b_q.shape, F32)] * 2,
        scratch_shapes=[pltpu.VMEM((tq // qb, 2 * qb, gw), BF16),
                        pltpu.VMEM((tq // qb, 2 * qb, gw), BF16),
                        pltpu.VMEM((n_unit, qb, 2 * qb), F32),
                        pltpu.VMEM((n_unit, qb, 2 * qb), BF16),
                        pltpu.VMEM((2 * n_unit, qb, LANES), F32)],
        compiler_params=_params(3),
        name=f"dilated_d{dil}",
    )(b_q, b_k, b_k, b_v, b_v)


def _gla_body(q_ref, k_ref, v_ref, g_ref, la_ref, ng_ref, o_ref,
              st_ref, b_ref, qh_ref, kd_ref, ke_ref, dec_ref, at_ref, oi_ref, kv_ref, *, n_sub):
    cs = C_CHUNK
    bsz = q_ref.shape[0]

    @pl.when(pl.program_id(0) == 0)
    def _():
        st_ref[...] = jnp.zeros(st_ref.shape, F32)

    r_i = lax.broadcasted_iota(jnp.int32, (cs, cs), 0)
    c_i = lax.broadcasted_iota(jnp.int32, (cs, cs), 1)
    tril = r_i >= c_i
    tri_ones = jnp.where(tril, 1.0, 0.0).astype(BF16)
    ng = ng_ref[...]
    units = [(c, bi) for c in range(n_sub) for bi in range(bsz)]
    rows_of = lambda c: slice(c * cs, (c + 1) * cs)
    pair_of = lambda h: slice((h // 2) * LANES, (h // 2 + 1) * LANES)
    v_of = lambda h: slice(h * C_HEAD_V, (h + 1) * C_HEAD_V)

    for u, (c, bi) in enumerate(units):
        la = la_ref[bi, rows_of(c), :]
        la_hi = la.astype(BF16)
        la_lo = (la - la_hi.astype(F32)).astype(BF16)
        b_ref[u] = _dot(tri_ones, la_hi) + _dot(tri_ones, la_lo)
    for u, (c, bi) in enumerate(units):
        b = b_ref[u]
        b_last = b[cs - 1:cs, :]
        k = k_ref[bi, rows_of(c), :]
        q_dec = q_ref[bi, rows_of(c), :] * jnp.exp(b)
        for h in range(C_HEADS):
            q_pair = q_dec[:, pair_of(h)]
            qh_ref[u * C_HEADS + h] = jnp.where(_half_mask(q_pair.shape, h % 2 == 1), q_pair,
                                                0.0).astype(BF16)
        kd_ref[u] = (k * jnp.exp(-b)).astype(BF16)
        ke_ref[u] = (k * jnp.exp(b_last - b)).astype(BF16)
        dec_ref[u] = jnp.exp(b_last)
    for u, (c, bi) in enumerate(units):
        for h in range(C_HEADS):
            attn = jnp.where(tril, _dot_nt(qh_ref[u * C_HEADS + h], kd_ref[u, :, pair_of(h)]), 0.0)
            at_ref[u * C_HEADS + h] = attn.astype(BF16)
    for u, (c, bi) in enumerate(units):
        for h in range(C_HEADS):
            v_h = v_ref[bi, rows_of(c), v_of(h)].astype(BF16)
            oi_ref[u * C_HEADS + h] = _dot(at_ref[u * C_HEADS + h], v_h)
            kv_ref[u * C_HEADS + h] = _dot_tn(v_h, ke_ref[u, :, pair_of(h)])
    for u, (c, bi) in enumerate(units):
        o_heads = []
        for h in range(C_HEADS):
            st = st_ref[bi * C_HEADS + h]
            o_h = oi_ref[u * C_HEADS + h] + _dot_nt(qh_ref[u * C_HEADS + h], st.astype(BF16))
            st_ref[bi * C_HEADS + h] = st * dec_ref[u, :, pair_of(h)] + kv_ref[u * C_HEADS + h]
            gate = g_ref[bi, rows_of(c), v_of(h)]
            o_heads.append(_rms(o_h, ng) * _silu(gate))
        o_ref[bi, rows_of(c), :] = jnp.concatenate(o_heads, axis=1).astype(BF16)


def _gla(c_q, c_k, c_v, c_g, la, norm_g, *, n_sub=8):
    bsz, seq, _ = c_q.shape
    rows = n_sub * C_CHUNK
    n_unit = n_sub * bsz
    blk = lambda n: (0, n, 0)
    kw, vw = C_HEADS * C_HEAD_K, C_HEADS * C_HEAD_V
    return pl.pallas_call(
        functools.partial(_gla_body, n_sub=n_sub),
        grid=(seq // rows,),
        in_specs=[pl.BlockSpec((bsz, rows, kw), blk), pl.BlockSpec((bsz, rows, kw), blk),
                  pl.BlockSpec((bsz, rows, vw), blk), pl.BlockSpec((bsz, rows, vw), blk),
                  pl.BlockSpec((bsz, rows, kw), blk), _resident(norm_g.shape)],
        out_specs=pl.BlockSpec((bsz, rows, vw), blk),
        out_shape=jax.ShapeDtypeStruct((bsz, seq, vw), BF16),
        scratch_shapes=[pltpu.VMEM((bsz * C_HEADS, C_HEAD_V, LANES), F32),
                        pltpu.VMEM((n_unit, C_CHUNK, kw), F32),
                        pltpu.VMEM((n_unit * C_HEADS, C_CHUNK, LANES), BF16),
                        pltpu.VMEM((n_unit, C_CHUNK, kw), BF16),
                        pltpu.VMEM((n_unit, C_CHUNK, kw), BF16),
                        pltpu.VMEM((n_unit, 1, kw), F32),
                        pltpu.VMEM((n_unit * C_HEADS, C_CHUNK, C_CHUNK), BF16),
                        pltpu.VMEM((n_unit * C_HEADS, C_CHUNK, C_HEAD_V), F32),
                        pltpu.VMEM((n_unit * C_HEADS, C_HEAD_V, LANES), F32)],
        compiler_params=_params(1),
        name="gla",
    )(c_q, c_k, c_v, c_g, la, norm_g)


def _mixer_out_body(h_ref, ya_ref, ob0_ref, ob1_ref, ob2_ref, ls0_ref, ls1_ref, ls2_ref, yc_ref,
                    pre_g_ref, wg_ref, bg_ref, wa_ref, wb_ref, wc_ref, wo_ref, post_g_ref, o_ref,
                    stage_ref):
    x = h_ref[...]
    tm, d = x.shape
    u = _rms(x, pre_g_ref[...]).astype(BF16)

    def token_order(blk_ref, slot):
        dil = blk_ref.shape[1]
        if dil == 1:
            return blk_ref[0, 0]
        n_l = blk_ref.shape[3] // LANES
        for r in range(dil):
            for j in range(n_l):
                stage_ref[slot * n_l + j, pl.ds(r, tm // dil, stride=dil), :] = (
                    blk_ref[0, r, :, j * LANES:(j + 1) * LANES])
        return jnp.concatenate([stage_ref[slot * n_l + j] for j in range(n_l)], axis=1)

    ls = [token_order(r, j) for j, r in enumerate((ls0_ref, ls1_ref, ls2_ref))]
    ob = [token_order(r, 3 + j) for j, r in enumerate((ob0_ref, ob1_ref, ob2_ref))]
    mx = jnp.maximum(jnp.maximum(ls[0], ls[1]), ls[2])
    ew = [jnp.exp(v - mx) for v in ls]
    y_b = ((ew[0] * ob[0] + ew[1] * ob[1] + ew[2] * ob[2]) / (ew[0] + ew[1] + ew[2])).astype(BF16)

    branches = ((ya_ref[...], wa_ref), (y_b, wb_ref), (yc_ref[...], wc_ref))
    merged = jnp.zeros(x.shape, F32)
    for i, (y, w_ref) in enumerate(branches):
        gate = jax.nn.sigmoid(_dot(u, wg_ref[:, i * d:(i + 1) * d]) + bg_ref[:, i * d:(i + 1) * d])
        merged = merged + gate * _dot(y, w_ref[...])
    m = _dot(merged.astype(BF16), wo_ref[...])
    o_ref[...] = x + _rms(m, post_g_ref[...])


def _mixer_out(h, y_a, o_b, lse_b, y_c, pre_g, w_gate, b_gate, w_a, w_b, w_c, w_o, post_g, *, seq,
               tm=512):
    t, d = h.shape
    tiles = seq // tm
    row = lambda i: (i, 0)

    def tile(a):
        if a.ndim == 2:
            return pl.BlockSpec((tm, a.shape[1]), row)
        dil, gw = a.shape[1], a.shape[3]
        return pl.BlockSpec((1, dil, tm // dil, gw), lambda i: (i // tiles, 0, i % tiles, 0))

    acts = [h, y_a, *o_b, *lse_b, y_c]
    consts = [pre_g, w_gate, b_gate, w_a, w_b, w_c, w_o, post_g]
    return pl.pallas_call(
        _mixer_out_body,
        grid=(t // tm,),
        in_specs=[tile(a) for a in acts] + [_resident(c.shape) for c in consts],
        out_specs=pl.BlockSpec((tm, d), row),
        out_shape=jax.ShapeDtypeStruct((t, d), F32),
        scratch_shapes=[pltpu.VMEM((2 * len(o_b) * o_b[0].shape[3] // LANES, tm, LANES), F32)],
        compiler_params=_params(1),
        name="mixer_out",
    )(*acts, *consts)


def _rope_tables(positions):
    half = ROT_DIM // 2
    inv_freq = ROPE_THETA ** (-jnp.arange(0, ROT_DIM, 2, dtype=F32) / ROT_DIM)
    ang = positions.astype(F32).reshape(-1, 1) * inv_freq
    cos, sin = jnp.cos(ang), jnp.sin(ang)
    t = cos.shape[0]
    pad = lambda n: jnp.zeros((t, n), F32)
    cos_t = jnp.concatenate([cos, cos, jnp.ones((t, HEAD_DIM - 2 * half), F32)], axis=1)
    sinp_t = jnp.concatenate([pad(half), sin, pad(HEAD_DIM - 2 * half)], axis=1)
    sinm_t = jnp.concatenate([-sin, pad(HEAD_DIM - half)], axis=1)
    rep = LANES // HEAD_DIM
    return tuple(jnp.tile(a, (1, rep)) for a in (cos_t, sinp_t, sinm_t))


def _split_w_in(w_in):
    d = w_in.shape[0]
    sizes = (A_HEADS * HEAD_DIM, A_KV_RANK, IDX_HEADS * IDX_DIM, IDX_DIM, IDX_HEADS,
             3 * B_HEADS * HEAD_DIM, C_HEADS * C_HEAD_K, C_HEADS * C_HEAD_K,
             C_HEADS * C_HEAD_V, C_HEADS * C_HEAD_V, C_ALPHA_RANK)
    names = ("a_q", "ckv", "i_q", "i_k", "i_w", "b_qkv", "c_q", "c_k", "c_v", "c_g", "c_a")
    parts, o = {}, 0
    for name, n in zip(names, sizes):
        parts[name] = w_in[:, o:o + n]
        o += n
    bw = B_HEADS * HEAD_DIM
    parts["b_q"], parts["b_k"], parts["b_v"] = (parts["b_qkv"][:, j * bw:(j + 1) * bw] for j in range(3))
    used = IDX_DIM + IDX_HEADS + C_ALPHA_RANK
    parts["small"] = jnp.concatenate(
        [parts["i_k"], parts["i_w"], parts["c_a"], jnp.zeros((d, LANES - used), w_in.dtype)], axis=1)
    return jnp.concatenate([parts[name] for name, _ in _IN_COLS], axis=1).astype(BF16)


def kernel(x, positions, ffn1_pre_g, ffn1_w_in, ffn1_w_out, ffn1_post_g, mix_pre_g, w_in, a_kv_norm_g, a_w_kv_up, c_w_alpha_up, c_b_alpha, c_norm_g, w_branch_a, w_branch_b, w_branch_c, w_gate, b_gate, w_out, mix_post_g, ffn2_pre_g, ffn2_w_in, ffn2_w_out, ffn2_post_g):
    bsz, seq, d = x.shape
    depth = w_in.shape[0]
    top_k = min(TOPK_MAX, seq // 4)
    tables = _rope_tables(positions)
    row = lambda a: a.reshape(1, -1)
    h = x.reshape(bsz * seq, d)
    for l in range(depth):
        h = _ffn(h, row(ffn1_pre_g[l]), ffn1_w_in[l].astype(BF16), ffn1_w_out[l].astype(BF16),
                 row(ffn1_post_g[l]))

        kv_up = a_w_kv_up[l].reshape(A_KV_RANK, A_HEADS, 2, HEAD_DIM)
        w_kup = kv_up[:, :, 0].reshape(A_KV_RANK, A_HEADS * HEAD_DIM).astype(BF16)
        w_vup_t = kv_up[:, :, 1].reshape(A_KV_RANK, A_HEADS // 2, LANES).transpose(1, 2, 0)
        w_vup_t = jnp.pad(w_vup_t, ((0, 0), (0, VT_ROWS - LANES), (0, 0)))
        w_vup_t = w_vup_t.reshape(A_HEADS // 2 * VT_ROWS, A_KV_RANK).astype(BF16)
        w_alpha = jnp.zeros((LANES, C_HEADS * C_HEAD_K), F32)
        w_alpha = w_alpha.at[SMALL_CA:SMALL_CA + C_ALPHA_RANK].set(c_w_alpha_up[l]).astype(BF16)
        p = _mixer_in(h, row(mix_pre_g[l]), _split_w_in(w_in[l]), *tables, row(a_kv_norm_g[l]),
                      w_kup, w_vup_t, w_alpha, row(c_b_alpha[l]), seq=seq)

        seq3 = lambda a: a.reshape(bsz, seq, a.shape[-1])
        v_t = p["v_t"].reshape(bsz, seq // DSA_KEY_CHUNK, -1, DSA_KEY_CHUNK)
        y_a = _dsa(seq3(p["q_i"]), seq3(p["small"]), seq3(p["q_a"]), seq3(p["k_i"]), seq3(p["k_a"]),
                   v_t, top_k=top_k).reshape(bsz * seq, -1)
        o_b, lse_b = [], []
        for g in range(len(B_GROUPS)):
            o_g, lse_g = _dilated_group(p[f"b_q{g}"], p[f"b_k{g}"], p[f"b_v{g}"])
            o_b.append(o_g)
            lse_b.append(lse_g)
        y_c = _gla(seq3(p["c_q"]), seq3(p["c_k"]), seq3(p["c_v"]), seq3(p["c_g"]), seq3(p["la"]),
                   row(c_norm_g[l])).reshape(bsz * seq, -1)

        h = _mixer_out(h, y_a, o_b, lse_b, y_c, row(mix_pre_g[l]), w_gate[l].astype(BF16),
                       row(b_gate[l]), w_branch_a[l].astype(BF16), w_branch_b[l].astype(BF16),
                       w_branch_c[l].astype(BF16), w_out[l].astype(BF16), row(mix_post_g[l]), seq=seq)

        h = _ffn(h, row(ffn2_pre_g[l]), ffn2_w_in[l].astype(BF16), ffn2_w_out[l].astype(BF16),
                 row(ffn2_post_g[l]))
    return h.reshape(bsz, seq, d)
```

```python
import functools

import jax
import jax.numpy as jnp
from jax import lax
from jax.experimental import pallas as pl
from jax.experimental.pallas import tpu as pltpu

F32 = jnp.float32
BF16 = jnp.bfloat16

HEAD_DIM = 64
ROT_DIM = HEAD_DIM // 4
ROPE_THETA = 500000.0
RMS_EPS = 1e-6
MACARON_WEIGHT = 0.5

A_HEADS = 8
A_KV_RANK = 128
IDX_HEADS = 8
IDX_DIM = 64
TOPK_MAX = 256

B_GROUPS = ((128, 1), (512, 4), (2048, 16))
B_HEADS_PER_GROUP = 4
B_HEADS = B_HEADS_PER_GROUP * len(B_GROUPS)

C_HEADS = 4
C_HEAD_K = 64
C_HEAD_V = 128
C_ALPHA_RANK = 16
C_GATE_TAU = 16.0
C_CHUNK = 64

LANES = 128
MXU_TILE = 256
Q_BLOCK = 128
DSA_Q_BLOCK = 256
DSA_KEY_CHUNK = 512
BF16_SUBLANES = 16
VT_ROWS = LANES + BF16_SUBLANES
BISECT_WARMUP = 14
HI_MARGIN = 2.0 ** -10
BISECT_UNROLL = 2
BISECT_MAX_ITERS = 16
COUNT_SLABS = 16
MASKED_LOGIT = -1e30
LOG2_E = 1.4426950408889634
VMEM_LIMIT = 60 * 1024 * 1024

SMALL_IK = 0
SMALL_IW = IDX_DIM
SMALL_CA = IDX_DIM + IDX_HEADS


def _dot(a, b):
    return jnp.dot(a, b, preferred_element_type=F32)


def _dot_nt(a, b):
    return lax.dot_general(a, b, (((1,), (1,)), ((), ())), preferred_element_type=F32)


def _dot_tn(a, b):
    return lax.dot_general(a, b, (((0,), (0,)), ((), ())), preferred_element_type=F32)


def _rms(x, g):
    return x * lax.rsqrt(jnp.mean(x * x, axis=-1, keepdims=True) + RMS_EPS) * g


def _silu(x):
    return x * jax.nn.sigmoid(x)


def _params(n_grid):
    return pltpu.CompilerParams(dimension_semantics=("arbitrary",) * n_grid,
                                vmem_limit_bytes=VMEM_LIMIT)


def _resident(shape):
    zeros = (0,) * len(shape)
    return pl.BlockSpec(shape, lambda *_: zeros, pipeline_mode=pl.Buffered(1))


def _resident_layer(stacked_shape, layer):
    return pl.BlockSpec((None,) + tuple(stacked_shape[1:]), lambda *_: (layer, 0, 0),
                        pipeline_mode=pl.Buffered(1))


def _half_mask(shape, upper):
    lane = lax.broadcasted_iota(jnp.int32, shape, len(shape) - 1) % LANES
    return (lane >= HEAD_DIM) if upper else (lane < HEAD_DIM)


def _ffn_body(h_ref, pre_g_ref, w_in_ref, w_out_ref, post_g_ref, o_ref, *, d_ff, cuts):
    x = h_ref[...]
    xn = _rms(x, pre_g_ref[...]).astype(BF16)
    acc = jnp.zeros(x.shape, F32)
    for lo, hi in zip(cuts[:-1], cuts[1:]):
        gate = _dot(xn, w_in_ref[:, lo:hi])
        up = _dot(xn, w_in_ref[:, d_ff + lo:d_ff + hi])
        act = (_silu(gate) * up).astype(BF16)
        acc = acc + _dot(act, w_out_ref[lo:hi, :])
    o_ref[...] = x + MACARON_WEIGHT * _rms(acc, post_g_ref[...])


def _ffn(h, pre_g, w_in, w_out, post_g, layer, *, tm=512):
    t, d = h.shape
    d_ff = w_out.shape[1]
    mid = (d_ff // 2 + MXU_TILE - 1) // MXU_TILE * MXU_TILE
    cuts = (0, mid, d_ff) if 0 < mid < d_ff else (0, d_ff)
    row = lambda i: (i, 0)
    return pl.pallas_call(
        functools.partial(_ffn_body, d_ff=d_ff, cuts=cuts),
        grid=(t // tm,),
        in_specs=[pl.BlockSpec((tm, d), row), _resident((1, d)), _resident_layer(w_in.shape, layer),
                  _resident_layer(w_out.shape, layer), _resident((1, d))],
        out_specs=pl.BlockSpec((tm, d), row),
        out_shape=jax.ShapeDtypeStruct((t, d), F32),
        compiler_params=_params(1),
        name="ffn",
    )(h, pre_g, w_in, w_out, post_g)


_IN_COLS = (("a_q", 512), ("i_q", 512), ("b_q", 768), ("b_k", 768), ("b_v", 768),
            ("c_q", 256), ("c_k", 256), ("c_v", 512), ("c_g", 512), ("ckv", 128), ("small", 128))


def _in_col_offsets():
    offs, o = {}, 0
    for name, n in _IN_COLS:
        offs[name] = (o, o + n)
        o += n
    return offs, o


def _rope(x, cos_t, sinp_t, sinm_t):
    outs = []
    for i in range(x.shape[1] // LANES):
        xs = x[:, i * LANES:(i + 1) * LANES]
        outs.append(xs * cos_t + pltpu.roll(xs, 8, 1) * sinp_t + pltpu.roll(xs, LANES - 8, 1) * sinm_t)
    return outs[0] if len(outs) == 1 else jnp.concatenate(outs, axis=1)


def _log_sigmoid(x):
    return jnp.minimum(x, 0.0) - jnp.log1p(jnp.exp(-jnp.abs(x)))


def _mixer_in_body(h_ref, g_ref, w_ref, cos_ref, sinp_ref, sinm_ref, kvg_ref, wk_ref, wv_ref,
                   wal_ref, bal_ref,
                   qa_ref, ka_ref, vt_ref, qi_ref, ki_ref, small_ref,
                   bq0_ref, bq1_ref, bq2_ref, bk0_ref, bk1_ref, bk2_ref, bv0_ref, bv1_ref, bv2_ref,
                   cq_ref, ck_ref, cv_ref, cg_ref, la_ref, stage_ref):
    offs, _ = _in_col_offsets()
    u = _rms(h_ref[...], g_ref[...]).astype(BF16)
    cos_t, sinp_t, sinm_t = cos_ref[...], sinp_ref[...], sinm_ref[...]
    tm = h_ref.shape[0]
    gw = B_HEADS_PER_GROUP * HEAD_DIM

    def proj(name):
        lo, hi = offs[name]
        return _dot(u, w_ref[:, lo:hi])

    def emit_dilated(x, out_refs):
        for j in range(x.shape[1] // LANES):
            stage_ref[j] = x[:, j * LANES:(j + 1) * LANES]
        per_g = gw // LANES
        for g, (_, dil) in enumerate(B_GROUPS):
            for r in range(dil):
                rows = [stage_ref[g * per_g + j, pl.ds(r, tm // dil, stride=dil), :] for j in range(per_g)]
                out_refs[g][0, r] = jnp.concatenate(rows, axis=1).astype(BF16)

    rope = lambda x: _rope(x, cos_t, sinp_t, sinm_t)
    q_scale = HEAD_DIM ** -0.5
    qa_ref[...] = (rope(proj("a_q")) * (q_scale * LOG2_E)).astype(BF16)
    qi_ref[...] = (rope(proj("i_q")) * (IDX_DIM ** -0.5)).astype(BF16)
    emit_dilated(rope(proj("b_q")) * q_scale, (bq0_ref, bq1_ref, bq2_ref))
    emit_dilated(rope(proj("b_k")), (bk0_ref, bk1_ref, bk2_ref))
    emit_dilated(proj("b_v"), (bv0_ref, bv1_ref, bv2_ref))
    cq_ref[...] = proj("c_q") * (C_HEAD_K ** -0.5)
    ck_ref[...] = proj("c_k")
    cv_ref[...] = proj("c_v")
    cg_ref[...] = proj("c_g")

    ckv = _rms(proj("ckv"), kvg_ref[...]).astype(BF16)
    ka_ref[...] = rope(_dot(ckv, wk_ref[...])).astype(BF16)
    v_t = _dot_nt(wv_ref[...], ckv)
    slab_row = lax.broadcasted_iota(jnp.int32, v_t.shape, 0) % VT_ROWS
    vt_ref[0] = jnp.where(slab_row >= LANES, 1.0, v_t).astype(BF16)

    small = proj("small")
    small_r = rope(small)
    ki_ref[...] = jnp.where(_half_mask(small.shape, False), small_r,
                            pltpu.roll(small_r, HEAD_DIM, 1)).astype(BF16)
    small_ref[...] = small
    la = _dot(small.astype(BF16), wal_ref[...]) + bal_ref[...]
    la_ref[...] = _log_sigmoid(la) * (1.0 / C_GATE_TAU)


def _mixer_in(h, g, w_big, cos_t, sinp_t, sinm_t, kv_g, w_kup, w_vup_t, w_alpha, b_alpha, *, seq):
    t, d = h.shape
    tm = DSA_KEY_CHUNK
    tiles = seq // tm
    gw = B_HEADS_PER_GROUP * HEAD_DIM
    row = lambda i: (i, 0)
    out_defs = [("q_a", 512, BF16), ("k_a", 512, BF16), ("v_t", "vt", BF16), ("q_i", 512, BF16),
                ("k_i", 128, BF16), ("small", LANES, F32)]
    out_defs += [(f"b_{nm}{gi}", ("dil", dil), BF16) for nm in "qkv" for gi, (_, dil) in enumerate(B_GROUPS)]
    out_defs += [("c_q", 256, F32), ("c_k", 256, F32), ("c_v", 512, F32), ("c_g", 512, F32),
                 ("la", 256, F32)]
    out_specs, out_shape = [], []
    for _, n, dt in out_defs:
        if n == "vt":
            out_specs.append(pl.BlockSpec((1, w_vup_t.shape[0], tm), lambda i: (i, 0, 0)))
            out_shape.append(jax.ShapeDtypeStruct((t // tm, w_vup_t.shape[0], tm), dt))
        elif isinstance(n, tuple):
            dil = n[1]
            out_specs.append(pl.BlockSpec((1, dil, tm // dil, gw),
                                          lambda i: (i // tiles, 0, i % tiles, 0)))
            out_shape.append(jax.ShapeDtypeStruct((t // seq, dil, seq // dil, gw), dt))
        else:
            out_specs.append(pl.BlockSpec((tm, n), row))
            out_shape.append(jax.ShapeDtypeStruct((t, n), dt))
    w_vup = w_vup_t
    outs = pl.pallas_call(
        _mixer_in_body,
        grid=(t // tm,),
        in_specs=[pl.BlockSpec((tm, d), row), _resident((1, d)), _resident(w_big.shape),
                  pl.BlockSpec((tm, LANES), row), pl.BlockSpec((tm, LANES), row),
                  pl.BlockSpec((tm, LANES), row), _resident(kv_g.shape), _resident(w_kup.shape),
                  _resident(w_vup.shape), _resident(w_alpha.shape), _resident(b_alpha.shape)],
        out_specs=out_specs,
        out_shape=out_shape,
        scratch_shapes=[pltpu.VMEM((B_HEADS * HEAD_DIM // LANES, tm, LANES), F32)],
        compiler_params=_params(1),
        name="mixer_in",
    )(h, g, w_big, cos_t, sinp_t, sinm_t, kv_g, w_kup, w_vup, w_alpha, b_alpha)
    return {name: o for (name, _, _), o in zip(out_defs, outs)}


def _masked_pair_rows(x, n_heads):
    out = []
    for h in range(n_heads):
        pair = x[:, (h // 2) * LANES:(h // 2 + 1) * LANES]
        out.append(jnp.where(_half_mask(pair.shape, h % 2 == 1), pair, jnp.zeros_like(pair)))
    return out


def _dsa_body(qi_ref, small_ref, qa_ref, ki_ref, ka_ref, vt_ref, o_ref,
              score_ref, acc_ref, qm_ref, s0_ref, s1_ref, *, seq, top_k):
    kc = DSA_KEY_CHUNK
    qb = DSA_Q_BLOCK
    i = pl.program_id(1)
    n_chunks = (i * qb + qb + kc - 1) // kc
    q_pos = i * qb + lax.broadcasted_iota(jnp.int32, (1, qb), 1)
    key_off = lax.broadcasted_iota(jnp.int32, (kc, qb), 0)

    n_pairs = A_HEADS // 2
    qi_heads = _masked_pair_rows(qi_ref[0], IDX_HEADS)
    for pr in range(n_pairs):
        qm_ref[pr] = jnp.concatenate([qi_heads[2 * pr], qi_heads[2 * pr + 1]], axis=0)
    w_t = jnp.transpose(small_ref[0])[SMALL_IW:SMALL_IW + IDX_HEADS, :] * (IDX_HEADS ** -0.5)

    def skewed(first, stage, carry):
        def step(j, st):
            carry, aux0 = st
            aux1 = first(2 * j + 1, s1_ref)
            carry = stage(2 * j, s0_ref, carry, aux0)
            aux0 = first(2 * j + 2, s0_ref)
            return stage(2 * j + 1, s1_ref, carry, aux1), aux0
        return lax.fori_loop(0, (n_chunks + 1) // 2, step, (carry, first(0, s0_ref)))[0]

    def idx_logits_stage(c, s_ref):
        off = pl.multiple_of(jnp.minimum(c, n_chunks - 1) * kc, kc)
        k_c = ki_ref[0, pl.ds(off, kc), :]
        for pr in range(n_pairs):
            s_ref[pr] = _dot_nt(k_c, qm_ref[pr])
        return 0

    def score_stage(c, s_ref, carry, _):
        mx, mn = carry
        cc = jnp.minimum(c, n_chunks - 1)
        sc = jnp.zeros((kc, qb), F32)
        for h in range(IDX_HEADS):
            logit = s_ref[h // 2, :, (h % 2) * qb:(h % 2 + 1) * qb]
            sc = sc + jnp.maximum(logit, 0.0) * w_t[h:h + 1, :]
        causal = (cc * kc + key_off) <= q_pos
        score_ref[cc] = jnp.where(causal, sc, -jnp.inf)
        mx = jnp.maximum(mx, jnp.max(jnp.where(causal, sc, -jnp.inf), axis=0, keepdims=True))
        mn = jnp.minimum(mn, jnp.min(jnp.where(causal, sc, jnp.inf), axis=0, keepdims=True))
        return mx, mn

    row_max, row_min = skewed(
        idx_logits_stage, score_stage,
        (jnp.full((1, qb), -jnp.inf, F32), jnp.full((1, qb), jnp.inf, F32)))

    slab = kc // COUNT_SLABS

    def sweep(init, fold, per_slab):
        def step(c, acc):
            for j in range(COUNT_SLABS):
                acc = fold(acc, per_slab(score_ref[c, j * slab:(j + 1) * slab, :]))
            return acc
        return lax.fori_loop(0, n_chunks, step, jnp.full((slab, qb), init, F32))

    def count_ge(thr):
        acc = sweep(0.0, lambda a, v: a + v, lambda x: jnp.where(x >= thr, 1.0, 0.0))
        return jnp.sum(acc, axis=0, keepdims=True)

    def max_below(thr):
        acc = sweep(-jnp.inf, jnp.maximum, lambda x: jnp.where(x < thr, x, -jnp.inf))
        return jnp.max(acc, axis=0, keepdims=True)

    k_f = float(top_k)
    n_causal = (q_pos + 1).astype(F32)
    few = n_causal <= k_f
    lo0 = row_min
    hi0 = row_max + (jnp.abs(row_max) + 1.0) * HI_MARGIN

    def halve(st):
        lo, hi, c_lo, c_hi = st
        mid = lo + (hi - lo) * 0.5
        cnt = count_ge(mid)
        ge = cnt >= k_f
        return (jnp.where(ge, mid, lo), jnp.where(ge, hi, mid),
                jnp.where(ge, cnt, c_lo), jnp.where(ge, c_hi, cnt))

    def snap(st):
        lo, hi, c_lo, c_hi = st
        v = max_below(hi)
        cnt = count_ge(v)
        ge = jnp.logical_and(cnt >= k_f, jnp.logical_not(few))
        lower = jnp.logical_and(cnt < k_f, jnp.logical_not(few))
        return (jnp.where(ge, v, lo), jnp.where(lower, v, hi),
                jnp.where(ge, cnt, c_lo), jnp.where(lower, cnt, c_hi)), ge

    st0 = (lo0, hi0, n_causal, jnp.zeros((1, qb), F32))
    st1 = lax.fori_loop(0, BISECT_WARMUP, lambda _, st: halve(st), st0)

    def search_cond(carry):
        return jnp.logical_and(carry[0] < BISECT_MAX_ITERS, carry[2] > 0.5)

    def search_step(carry):
        it, st, _ = carry
        for _ in range(BISECT_UNROLL):
            st = halve(st)
        st, snapped = snap(st)
        done = few | (st[2] == k_f) | snapped
        return it + 1, st, jnp.max(jnp.where(done, 0.0, 1.0))

    pending0 = jnp.max(jnp.where(few | (st1[2] == k_f), 0.0, 1.0))
    _, (lo, hi, c_lo, c_hi), _ = lax.while_loop(search_cond, search_step, (jnp.int32(0), st1, pending0))

    tied = jnp.logical_and(jnp.logical_not(few), c_lo > k_f)
    need = k_f - c_hi

    @pl.when(jnp.max(jnp.where(tied, 1.0, 0.0)) > 0.5)
    def _():
        r_i = lax.broadcasted_iota(jnp.int32, (kc, kc), 0)
        c_i = lax.broadcasted_iota(jnp.int32, (kc, kc), 1)
        tri = jnp.where(r_i >= c_i, 1.0, 0.0).astype(BF16)

        def step(c, seen):
            x = score_ref[c]
            tie = tied & (x >= lo) & (x < hi)
            prefix = seen + _dot(tri, jnp.where(tie, 1.0, 0.0).astype(BF16))
            score_ref[c] = jnp.where(tie & (prefix > need), -jnp.inf, x)
            return prefix[kc - 1:kc, :]
        lax.fori_loop(0, n_chunks, step, jnp.zeros((1, qb), F32))

    qa_heads = _masked_pair_rows(qa_ref[0], A_HEADS)
    for pr in range(n_pairs):
        qm_ref[pr] = jnp.concatenate([qa_heads[2 * pr], qa_heads[2 * pr + 1]], axis=0)
    acc_ref[...] = jnp.zeros(acc_ref.shape, F32)

    def attn_logits_stage(c, s_ref):
        cc = jnp.minimum(c, n_chunks - 1)
        off = pl.multiple_of(cc * kc, kc)
        bias = jnp.where((score_ref[cc] >= lo) & (c < n_chunks), 0.0, MASKED_LOGIT)
        col_max = []
        for pr in range(n_pairs):
            s2 = _dot_nt(ka_ref[0, pl.ds(off, kc), pr * LANES:(pr + 1) * LANES], qm_ref[pr])
            for e in range(2):
                s = s2[:, e * qb:(e + 1) * qb] + bias
                s_ref[pr, :, e * qb:(e + 1) * qb] = s
                col_max.append(jnp.max(s, axis=0, keepdims=True))
        return tuple(col_max)

    def softmax_pv_stage(c, s_ref, m_all, col_max):
        cc = jnp.minimum(c, n_chunks - 1)
        m_out = []
        for pr in range(n_pairs):
            p_halves, alphas = [], []
            for e in range(2):
                h = 2 * pr + e
                m_new = jnp.maximum(m_all[h], col_max[h])
                alphas.append(jnp.exp2(m_all[h] - m_new))
                p_halves.append(jnp.exp2(s_ref[pr, :, e * qb:(e + 1) * qb] - m_new).astype(BF16))
                m_out.append(m_new)
            v_c = vt_ref[0, cc, pr * VT_ROWS:(pr + 1) * VT_ROWS, :]
            pv = _dot(v_c, jnp.concatenate(p_halves, axis=1))
            acc_ref[pr] = acc_ref[pr] * jnp.concatenate(alphas, axis=1) + pv
        return tuple(m_out)

    m0 = tuple(jnp.full((1, qb), MASKED_LOGIT, F32) for _ in range(A_HEADS))
    skewed(attn_logits_stage, softmax_pv_stage, m0)

    for pr in range(n_pairs):
        a = acc_ref[pr]
        denom = a[LANES:LANES + 1, :]
        o_t = jnp.concatenate([a[:HEAD_DIM, :qb] / denom[:, :qb],
                               a[HEAD_DIM:LANES, qb:] / denom[:, qb:]], axis=0)
        o_ref[0, :, pr * LANES:(pr + 1) * LANES] = jnp.transpose(o_t).astype(BF16)


def _dsa(q_i, small, q_a, k_i, k_a, v_t, *, top_k):
    bsz, seq, _ = q_a.shape
    qb, kc = DSA_Q_BLOCK, DSA_KEY_CHUNK
    blk = lambda b, i: (b, i, 0)
    whole = lambda shape: pl.BlockSpec((1,) + shape, lambda b, i: (b,) + (0,) * len(shape),
                                       pipeline_mode=pl.Buffered(1))
    return pl.pallas_call(
        functools.partial(_dsa_body, seq=seq, top_k=top_k),
        grid=(bsz, seq // qb),
        in_specs=[pl.BlockSpec((1, qb, 512), blk), pl.BlockSpec((1, qb, LANES), blk),
                  pl.BlockSpec((1, qb, 512), blk), whole((seq, LANES)), whole((seq, 512)),
                  whole(v_t.shape[1:])],
        out_specs=pl.BlockSpec((1, qb, 512), blk),
        out_shape=jax.ShapeDtypeStruct((bsz, seq, 512), BF16),
        scratch_shapes=[pltpu.VMEM((seq // kc, kc, qb), F32),
                        pltpu.VMEM((A_HEADS // 2, VT_ROWS, 2 * qb), F32),
                        pltpu.VMEM((A_HEADS // 2, 2 * qb, LANES), BF16),
                        pltpu.VMEM((A_HEADS // 2, kc, 2 * qb), F32),
                        pltpu.VMEM((A_HEADS // 2, kc, 2 * qb), F32)],
        compiler_params=_params(2),
        name="dsa",
    )(q_i, small, q_a, k_i, k_a, v_t)


def _dilated_body(q_ref, kp_ref, k_ref, vp_ref, v_ref, o_ref, lse_ref,
                  kk_ref, vv_ref, sc_ref, p_ref, stat_ref, *, tq):
    qb = Q_BLOCK
    n = pl.program_id(2)
    row = lax.broadcasted_iota(jnp.int32, (qb, 2 * qb), 0)
    col = lax.broadcasted_iota(jnp.int32, (qb, 2 * qb), 1)
    rel = row + qb - col
    band = (rel >= 0) & (rel <= qb)
    n_blk = tq // qb
    heads = B_HEADS_PER_GROUP
    pair_of = lambda h: slice((h // 2) * LANES, (h // 2 + 1) * LANES)

    for s in range(n_blk):
        rows = slice(s * qb, (s + 1) * qb)
        prev_rows = slice((s - 1) * qb, s * qb)
        kk_ref[s, :qb] = kp_ref[0, 0] if s == 0 else k_ref[0, 0, prev_rows, :]
        kk_ref[s, qb:] = k_ref[0, 0, rows, :]
        vv_ref[s, :qb] = vp_ref[0, 0] if s == 0 else v_ref[0, 0, prev_rows, :]
        vv_ref[s, qb:] = v_ref[0, 0, rows, :]
    for s in range(n_blk):
        valid = (band & ((col >= qb) | (n > 0))) if s == 0 else band
        for h in range(heads):
            q_pair = q_ref[0, 0, s * qb:(s + 1) * qb, pair_of(h)]
            q_h = jnp.where(_half_mask(q_pair.shape, h % 2 == 1), q_pair, jnp.zeros_like(q_pair))
            sc_ref[s * heads + h] = jnp.where(valid, _dot_nt(q_h, kk_ref[s, :, pair_of(h)]), -jnp.inf)
    for u in range(n_blk * heads):
        sc = sc_ref[u]
        m = jnp.max(sc, axis=1, keepdims=True)
        p = jnp.exp(sc - m)
        l = jnp.sum(p, axis=1, keepdims=True)
        p_ref[u] = p.astype(BF16)
        stat_ref[2 * u] = jnp.broadcast_to(l, (qb, LANES))
        stat_ref[2 * u + 1] = jnp.broadcast_to(m + jnp.log(l), (qb, LANES))
    lower = _half_mask((qb, LANES), False)
    for s in range(n_blk):
        o_parts, lse_parts = [], []
        for pr in range(heads // 2):
            o_h = []
            for h in (2 * pr, 2 * pr + 1):
                u = s * heads + h
                o_h.append(_dot(p_ref[u], vv_ref[s, :, pair_of(h)]) / stat_ref[2 * u])
            o_parts.append(jnp.where(lower, o_h[0], o_h[1]))
            lse_parts.append(jnp.where(lower, stat_ref[2 * (s * heads + 2 * pr) + 1],
                                       stat_ref[2 * (s * heads + 2 * pr + 1) + 1]))
        o_ref[0, 0, s * qb:(s + 1) * qb, :] = jnp.concatenate(o_parts, axis=1)
        lse_ref[0, 0, s * qb:(s + 1) * qb, :] = jnp.concatenate(lse_parts, axis=1)


def _dilated_group(b_q, b_k, b_v):
    bsz, dil, sub, gw = b_q.shape
    qb = Q_BLOCK
    tq = min(512, sub)
    n_unit = tq // qb * B_HEADS_PER_GROUP
    cur = lambda b, r, n: (b, r, n, 0)
    prev = lambda b, r, n: (b, r, jnp.maximum(n * (tq // qb) - 1, 0), 0)
    return pl.pallas_call(
        functools.partial(_dilated_body, tq=tq),
        grid=(bsz, dil, sub // tq),
        in_specs=[pl.BlockSpec((1, 1, tq, gw), cur),
                  pl.BlockSpec((1, 1, qb, gw), prev), pl.BlockSpec((1, 1, tq, gw), cur),
                  pl.BlockSpec((1, 1, qb, gw), prev), pl.BlockSpec((1, 1, tq, gw), cur)],
        out_specs=[pl.BlockSpec((1, 1, tq, gw), cur), pl.BlockSpec((1, 1, tq, gw), cur)],
        out_shape=[jax.ShapeDtypeStruct(b_q.shape, F32)] * 2,
        scratch_shapes=[pltpu.VMEM((tq // qb, 2 * qb, gw), BF16),
                        pltpu.VMEM((tq // qb, 2 * qb, gw), BF16),
                        pltpu.VMEM((n_unit, qb, 2 * qb), F32),
                        pltpu.VMEM((n_unit, qb, 2 * qb), BF16),
                        pltpu.VMEM((2 * n_unit, qb, LANES), F32)],
        compiler_params=_params(3),
        name=f"dilated_d{dil}",
    )(b_q, b_k, b_k, b_v, b_v)


def _gla_body(q_ref, k_ref, v_ref, g_ref, la_ref, ng_ref, o_ref,
              st_ref, b_ref, qh_ref, kd_ref, ke_ref, dec_ref, at_ref, oi_ref, kv_ref, *, n_sub):
    cs = C_CHUNK
    bsz = q_ref.shape[0]

    @pl.when(pl.program_id(0) == 0)
    def _():
        st_ref[...] = jnp.zeros(st_ref.shape, F32)

    r_i = lax.broadcasted_iota(jnp.int32, (cs, cs), 0)
    c_i = lax.broadcasted_iota(jnp.int32, (cs, cs), 1)
    tril = r_i >= c_i
    tri_ones = jnp.where(tril, 1.0, 0.0).astype(BF16)
    ng = ng_ref[...]
    units = [(c, bi) for c in range(n_sub) for bi in range(bsz)]
    rows_of = lambda c: slice(c * cs, (c + 1) * cs)
    pair_of = lambda h: slice((h // 2) * LANES, (h // 2 + 1) * LANES)
    v_of = lambda h: slice(h * C_HEAD_V, (h + 1) * C_HEAD_V)

    for u, (c, bi) in enumerate(units):
        la = la_ref[bi, rows_of(c), :]
        la_hi = la.astype(BF16)
        la_lo = (la - la_hi.astype(F32)).astype(BF16)
        b_ref[u] = _dot(tri_ones, la_hi) + _dot(tri_ones, la_lo)
    for u, (c, bi) in enumerate(units):
        b = b_ref[u]
        b_last = b[cs - 1:cs, :]
        k = k_ref[bi, rows_of(c), :]
        q_dec = q_ref[bi, rows_of(c), :] * jnp.exp(b)
        for h in range(C_HEADS):
            q_pair = q_dec[:, pair_of(h)]
            qh_ref[u * C_HEADS + h] = jnp.where(_half_mask(q_pair.shape, h % 2 == 1), q_pair,
                                                0.0).astype(BF16)
        kd_ref[u] = (k * jnp.exp(-b)).astype(BF16)
        ke_ref[u] = (k * jnp.exp(b_last - b)).astype(BF16)
        dec_ref[u] = jnp.exp(b_last)
    for u, (c, bi) in enumerate(units):
        for h in range(C_HEADS):
            attn = jnp.where(tril, _dot_nt(qh_ref[u * C_HEADS + h], kd_ref[u, :, pair_of(h)]), 0.0)
            at_ref[u * C_HEADS + h] = attn.astype(BF16)
    for u, (c, bi) in enumerate(units):
        for h in range(C_HEADS):
            v_h = v_ref[bi, rows_of(c), v_of(h)].astype(BF16)
            oi_ref[u * C_HEADS + h] = _dot(at_ref[u * C_HEADS + h], v_h)
            kv_ref[u * C_HEADS + h] = _dot_tn(v_h, ke_ref[u, :, pair_of(h)])
    for u, (c, bi) in enumerate(units):
        o_heads = []
        for h in range(C_HEADS):
            st = st_ref[bi * C_HEADS + h]
            o_h = oi_ref[u * C_HEADS + h] + _dot_nt(qh_ref[u * C_HEADS + h], st.astype(BF16))
            st_ref[bi * C_HEADS + h] = st * dec_ref[u, :, pair_of(h)] + kv_ref[u * C_HEADS + h]
            gate = g_ref[bi, rows_of(c), v_of(h)]
            o_heads.append(_rms(o_h, ng) * _silu(gate))
        o_ref[bi, rows_of(c), :] = jnp.concatenate(o_heads, axis=1).astype(BF16)


def _gla(c_q, c_k, c_v, c_g, la, norm_g, *, n_sub=8):
    bsz, seq, _ = c_q.shape
    rows = n_sub * C_CHUNK
    n_unit = n_sub * bsz
    blk = lambda n: (0, n, 0)
    kw, vw = C_HEADS * C_HEAD_K, C_HEADS * C_HEAD_V
    return pl.pallas_call(
        functools.partial(_gla_body, n_sub=n_sub),
        grid=(seq // rows,),
        in_specs=[pl.BlockSpec((bsz, rows, kw), blk), pl.BlockSpec((bsz, rows, kw), blk),
                  pl.BlockSpec((bsz, rows, vw), blk), pl.BlockSpec((bsz, rows, vw), blk),
                  pl.BlockSpec((bsz, rows, kw), blk), _resident(norm_g.shape)],
        out_specs=pl.BlockSpec((bsz, rows, vw), blk),
        out_shape=jax.ShapeDtypeStruct((bsz, seq, vw), BF16),
        scratch_shapes=[pltpu.VMEM((bsz * C_HEADS, C_HEAD_V, LANES), F32),
                        pltpu.VMEM((n_unit, C_CHUNK, kw), F32),
                        pltpu.VMEM((n_unit * C_HEADS, C_CHUNK, LANES), BF16),
                        pltpu.VMEM((n_unit, C_CHUNK, kw), BF16),
                        pltpu.VMEM((n_unit, C_CHUNK, kw), BF16),
                        pltpu.VMEM((n_unit, 1, kw), F32),
                        pltpu.VMEM((n_unit * C_HEADS, C_CHUNK, C_CHUNK), BF16),
                        pltpu.VMEM((n_unit * C_HEADS, C_CHUNK, C_HEAD_V), F32),
                        pltpu.VMEM((n_unit * C_HEADS, C_HEAD_V, LANES), F32)],
        compiler_params=_params(1),
        name="gla",
    )(c_q, c_k, c_v, c_g, la, norm_g)


def _mixer_out_body(h_ref, ya_ref, ob0_ref, ob1_ref, ob2_ref, ls0_ref, ls1_ref, ls2_ref, yc_ref,
                    pre_g_ref, wg_ref, bg_ref, wa_ref, wb_ref, wc_ref, wo_ref, post_g_ref, o_ref,
                    stage_ref):
    x = h_ref[...]
    tm, d = x.shape
    u = _rms(x, pre_g_ref[...]).astype(BF16)

    def token_order(blk_ref, slot):
        dil = blk_ref.shape[1]
        if dil == 1:
            return blk_ref[0, 0]
        n_l = blk_ref.shape[3] // LANES
        for r in range(dil):
            for j in range(n_l):
                stage_ref[slot * n_l + j, pl.ds(r, tm // dil, stride=dil), :] = (
                    blk_ref[0, r, :, j * LANES:(j + 1) * LANES])
        return jnp.concatenate([stage_ref[slot * n_l + j] for j in range(n_l)], axis=1)

    ls = [token_order(r, j) for j, r in enumerate((ls0_ref, ls1_ref, ls2_ref))]
    ob = [token_order(r, 3 + j) for j, r in enumerate((ob0_ref, ob1_ref, ob2_ref))]
    mx = jnp.maximum(jnp.maximum(ls[0], ls[1]), ls[2])
    ew = [jnp.exp(v - mx) for v in ls]
    y_b = ((ew[0] * ob[0] + ew[1] * ob[1] + ew[2] * ob[2]) / (ew[0] + ew[1] + ew[2])).astype(BF16)

    branches = ((ya_ref[...], wa_ref), (y_b, wb_ref), (yc_ref[...], wc_ref))
    merged = jnp.zeros(x.shape, F32)
    for i, (y, w_ref) in enumerate(branches):
        gate = jax.nn.sigmoid(_dot(u, wg_ref[:, i * d:(i + 1) * d]) + bg_ref[:, i * d:(i + 1) * d])
        merged = merged + gate * _dot(y, w_ref[...])
    m = _dot(merged.astype(BF16), wo_ref[...])
    o_ref[...] = x + _rms(m, post_g_ref[...])


def _mixer_out(h, y_a, o_b, lse_b, y_c, pre_g, w_gate, b_gate, w_a, w_b, w_c, w_o, post_g, *, seq,
               tm=512):
    t, d = h.shape
    tiles = seq // tm
    row = lambda i: (i, 0)

    def tile(a):
        if a.ndim == 2:
            return pl.BlockSpec((tm, a.shape[1]), row)
        dil, gw = a.shape[1], a.shape[3]
        return pl.BlockSpec((1, dil, tm // dil, gw), lambda i: (i // tiles, 0, i % tiles, 0))

    acts = [h, y_a, *o_b, *lse_b, y_c]
    consts = [pre_g, w_gate, b_gate, w_a, w_b, w_c, w_o, post_g]
    return pl.pallas_call(
        _mixer_out_body,
        grid=(t // tm,),
        in_specs=[tile(a) for a in acts] + [_resident(c.shape) for c in consts],
        out_specs=pl.BlockSpec((tm, d), row),
        out_shape=jax.ShapeDtypeStruct((t, d), F32),
        scratch_shapes=[pltpu.VMEM((2 * len(o_b) * o_b[0].shape[3] // LANES, tm, LANES), F32)],
        compiler_params=_params(1),
        name="mixer_out",
    )(*acts, *consts)


def _rope_tables(positions):
    half = ROT_DIM // 2
    inv_freq = ROPE_THETA ** (-jnp.arange(0, ROT_DIM, 2, dtype=F32) / ROT_DIM)
    ang = positions.astype(F32).reshape(-1, 1) * inv_freq
    cos, sin = jnp.cos(ang), jnp.sin(ang)
    t = cos.shape[0]
    pad = lambda n: jnp.zeros((t, n), F32)
    cos_t = jnp.concatenate([cos, cos, jnp.ones((t, HEAD_DIM - 2 * half), F32)], axis=1)
    sinp_t = jnp.concatenate([pad(half), sin, pad(HEAD_DIM - 2 * half)], axis=1)
    sinm_t = jnp.concatenate([-sin, pad(HEAD_DIM - half)], axis=1)
    rep = LANES // HEAD_DIM
    return tuple(jnp.tile(a, (1, rep)) for a in (cos_t, sinp_t, sinm_t))


def _split_w_in(w_in):
    d = w_in.shape[0]
    sizes = (A_HEADS * HEAD_DIM, A_KV_RANK, IDX_HEADS * IDX_DIM, IDX_DIM, IDX_HEADS,
             3 * B_HEADS * HEAD_DIM, C_HEADS * C_HEAD_K, C_HEADS * C_HEAD_K,
             C_HEADS * C_HEAD_V, C_HEADS * C_HEAD_V, C_ALPHA_RANK)
    names = ("a_q", "ckv", "i_q", "i_k", "i_w", "b_qkv", "c_q", "c_k", "c_v", "c_g", "c_a")
    parts, o = {}, 0
    for name, n in zip(names, sizes):
        parts[name] = w_in[:, o:o + n]
        o += n
    bw = B_HEADS * HEAD_DIM
    parts["b_q"], parts["b_k"], parts["b_v"] = (parts["b_qkv"][:, j * bw:(j + 1) * bw] for j in range(3))
    used = IDX_DIM + IDX_HEADS + C_ALPHA_RANK
    parts["small"] = jnp.concatenate(
        [parts["i_k"], parts["i_w"], parts["c_a"], jnp.zeros((d, LANES - used), w_in.dtype)], axis=1)
    return jnp.concatenate([parts[name] for name, _ in _IN_COLS], axis=1).astype(BF16)


def kernel(x, positions, ffn1_pre_g, ffn1_w_in, ffn1_w_out, ffn1_post_g, mix_pre_g, w_in, a_kv_norm_g, a_w_kv_up, c_w_alpha_up, c_b_alpha, c_norm_g, w_branch_a, w_branch_b, w_branch_c, w_gate, b_gate, w_out, mix_post_g, ffn2_pre_g, ffn2_w_in, ffn2_w_out, ffn2_post_g):
    bsz, seq, d = x.shape
    depth = w_in.shape[0]
    top_k = min(TOPK_MAX, seq // 4)
    tables = _rope_tables(positions)
    row = lambda a: a.reshape(1, -1)
    h = x.reshape(bsz * seq, d)
    ffn1_w = (ffn1_w_in.astype(BF16), ffn1_w_out.astype(BF16))
    ffn2_w = (ffn2_w_in.astype(BF16), ffn2_w_out.astype(BF16))
    for l in range(depth):
        h = _ffn(h, row(ffn1_pre_g[l]), *ffn1_w, row(ffn1_post_g[l]), l)

        kv_up = a_w_kv_up[l].reshape(A_KV_RANK, A_HEADS, 2, HEAD_DIM)
        w_kup = kv_up[:, :, 0].reshape(A_KV_RANK, A_HEADS * HEAD_DIM).astype(BF16)
        w_vup_t = kv_up[:, :, 1].reshape(A_KV_RANK, A_HEADS // 2, LANES).transpose(1, 2, 0)
        w_vup_t = jnp.pad(w_vup_t, ((0, 0), (0, VT_ROWS - LANES), (0, 0)))
        w_vup_t = w_vup_t.reshape(A_HEADS // 2 * VT_ROWS, A_KV_RANK).astype(BF16)
        w_alpha = jnp.zeros((LANES, C_HEADS * C_HEAD_K), F32)
        w_alpha = w_alpha.at[SMALL_CA:SMALL_CA + C_ALPHA_RANK].set(c_w_alpha_up[l]).astype(BF16)
        p = _mixer_in(h, row(mix_pre_g[l]), _split_w_in(w_in[l]), *tables, row(a_kv_norm_g[l]),
                      w_kup, w_vup_t, w_alpha, row(c_b_alpha[l]), seq=seq)

        seq3 = lambda a: a.reshape(bsz, seq, a.shape[-1])
        v_t = p["v_t"].reshape(bsz, seq // DSA_KEY_CHUNK, -1, DSA_KEY_CHUNK)
        y_a = _dsa(seq3(p["q_i"]), seq3(p["small"]), seq3(p["q_a"]), seq3(p["k_i"]), seq3(p["k_a"]),
                   v_t, top_k=top_k).reshape(bsz * seq, -1)
        o_b, lse_b = [], []
        for g in range(len(B_GROUPS)):
            o_g, lse_g = _dilated_group(p[f"b_q{g}"], p[f"b_k{g}"], p[f"b_v{g}"])
            o_b.append(o_g)
            lse_b.append(lse_g)
        y_c = _gla(seq3(p["c_q"]), seq3(p["c_k"]), seq3(p["c_v"]), seq3(p["c_g"]), seq3(p["la"]),
                   row(c_norm_g[l])).reshape(bsz * seq, -1)

        h = _mixer_out(h, y_a, o_b, lse_b, y_c, row(mix_pre_g[l]), w_gate[l].astype(BF16),
                       row(b_gate[l]), w_branch_a[l].astype(BF16), w_branch_b[l].astype(BF16),
                       w_branch_c[l].astype(BF16), w_out[l].astype(BF16), row(mix_post_g[l]), seq=seq)

        h = _ffn(h, row(ffn2_pre_g[l]), *ffn2_w, row(ffn2_post_g[l]), l)
    return h.reshape(bsz, seq, d)
```
